```python
import jax
import jax.numpy as jnp
from jax import lax
import numpy as np

D_MODEL = 1024
BATCH = 2
SEQ = 8192
DEPTH = 1

D_MIX = D_MODEL
D_POOL = D_MIX // 2
D_ATTN = D_MIX - D_POOL
POOL_WINDOWS = (2, 4, 8, 16)
N_POOL_GROUPS = len(POOL_WINDOWS)
POOL_GROUP_DIM = D_POOL // N_POOL_GROUPS
HEAD_DIM = 64
N_HEADS = D_ATTN // HEAD_DIM
DILATION_PAIRS = ((128, 1), (512, 4), (2048, 16))
Q_BLOCK = 128
D_PROJ_IN = D_POOL + 3 * D_ATTN
D_FF = 2816
N_MOD = 9
EPS = 1e-6

kernel_name = "hybrid_pool_dilated_attn_macaron_block"


def rmsnorm(x, g):
    xf = x.astype(jnp.float32)
    y = xf * lax.rsqrt(jnp.mean(xf * xf, axis=-1, keepdims=True) + EPS)
    return (y * g.astype(jnp.float32)).astype(x.dtype)


def modulate(n, shift, scale):
    return n * (1 + scale) + shift


def swiglu(n, w_gate, w_up, w_down):
    return (jax.nn.silu(n @ w_gate) * (n @ w_up)) @ w_down


def alibi_slopes(n_heads):
    return jnp.exp2(-8.0 * jnp.arange(1, n_heads + 1, dtype=jnp.float32) / n_heads)


def multiscale_pool(u, w_pool, pool_scale):
    B, S, C = u.shape
    uf = u.astype(jnp.float32)
    cs0 = jnp.pad(jnp.cumsum(uf, axis=1), ((0, 0), (1, 0), (0, 0)))
    t = jnp.arange(S)
    groups = []
    for g, w in enumerate(POOL_WINDOWS):
        sl = slice(g * POOL_GROUP_DIM, (g + 1) * POOL_GROUP_DIM)
        csg = cs0[..., sl]
        lower = jnp.pad(csg[:, :S + 1 - w], ((0, 0), (w - 1, 0), (0, 0)))
        count = jnp.minimum(t + 1, w).astype(jnp.float32)[None, :, None]
        mean = (csg[:, 1:] - lower) / count
        groups.append(mean - uf[..., sl])
    pooled = jnp.stack(groups, axis=2)
    y = jnp.einsum('bsgc,gcd->bsgd', pooled, w_pool.astype(jnp.float32)).reshape(B, S, C)
    return (y * pool_scale.astype(jnp.float32)).astype(u.dtype)


def dilated_attention(q, k, v):
    B, S, H, Dh = q.shape
    n_blk = S // Q_BLOCK
    slopes = alibi_slopes(H)
    scale = Dh ** -0.5

    def block(i):
        t0 = i * Q_BLOCK
        t = t0 + jnp.arange(Q_BLOCK)
        qb = lax.dynamic_slice_in_dim(q, t0, Q_BLOCK, axis=1).astype(jnp.float32) * scale
        mxs, dens, nums = [], [], []
        for window, dil in DILATION_PAIRS:
            dist = dil * jnp.arange(window // dil + 1)
            idx = t[:, None] - dist[None, :]
            valid = idx >= 0
            idx = jnp.maximum(idx, 0)
            kg = jnp.take(k, idx, axis=1).astype(jnp.float32)
            vg = jnp.take(v, idx, axis=1).astype(jnp.float32)
            s = jnp.einsum('bqhd,bqjhd->bqhj', qb, kg)
            s = s - slopes[:, None] * dist.astype(jnp.float32)[None, :]
            s = jnp.where(valid[None, :, None, :], s, -jnp.inf)
            mx = jnp.max(s, axis=-1)
            p = jnp.exp(s - mx[..., None])
            dens.append(jnp.sum(p, axis=-1))
            nums.append(jnp.einsum('bqhj,bqjhd->bqhd', p, vg))
            mxs.append(mx)
        m_all = jnp.stack(mxs)
        w_r = jnp.exp(m_all - jnp.max(m_all, axis=0))
        num = sum(w_r[r][..., None] * nums[r] for r in range(len(DILATION_PAIRS)))
        den = sum(w_r[r] * dens[r] for r in range(len(DILATION_PAIRS)))
        return (num / den[..., None]).astype(q.dtype)

    out = lax.map(block, jnp.arange(n_blk))
    return jnp.moveaxis(out, 0, 1).reshape(B, S, H, Dh)


def hybrid_mixer(n, w_in, w_pool, pool_scale, w_out):
    B, S, _ = n.shape
    z = n @ w_in
    u, q, k, v = jnp.split(z, [D_POOL, D_POOL + D_ATTN, D_POOL + 2 * D_ATTN], axis=-1)
    y_pool = multiscale_pool(u, w_pool, pool_scale)
    hs = (B, S, N_HEADS, HEAD_DIM)
    y_attn = dilated_attention(q.reshape(hs), k.reshape(hs), v.reshape(hs)).reshape(B, S, D_ATTN)
    return jnp.concatenate([y_pool, y_attn], axis=-1) @ w_out


def setup_inputs(seed: int = 0) -> dict:
    key = jax.random.key(seed)
    ks = jax.random.split(key, 20)
    f32 = jnp.float32
    L, D = DEPTH, D_MODEL

    def nrm(k, shape, fan_in, mult=1.0):
        return jax.random.normal(k, shape, f32) * (mult * fan_in ** -0.5)

    def gain(k, shape):
        return 1.0 + 0.05 * jax.random.normal(k, shape, f32)

    return {
        "x": jax.random.normal(ks[0], (BATCH, SEQ, D), f32),
        "c": jax.random.normal(ks[1], (BATCH, D), f32),
        "w_ada": nrm(ks[2], (L, D, N_MOD * D), D, 0.5),
        "b_ada": 0.02 * jax.random.normal(ks[3], (L, N_MOD * D), f32),
        "g_ffn1": gain(ks[4], (L, D)),
        "w1_gate": nrm(ks[5], (L, D, D_FF), D),
        "w1_up": nrm(ks[6], (L, D, D_FF), D),
        "w1_down": nrm(ks[7], (L, D_FF, D), D_FF),
        "g_mix": gain(ks[8], (L, D)),
        "w_in": nrm(ks[9], (L, D, D_PROJ_IN), D),
        "w_pool": nrm(ks[10], (L, N_POOL_GROUPS, POOL_GROUP_DIM, POOL_GROUP_DIM), POOL_GROUP_DIM),
        "pool_scale": gain(ks[11], (L, D_POOL)),
        "w_out": nrm(ks[12], (L, D_MIX, D), D_MIX),
        "g_ffn2": gain(ks[13], (L, D)),
        "w2_gate": nrm(ks[14], (L, D, D_FF), D),
        "w2_up": nrm(ks[15], (L, D, D_FF), D),
        "w2_down": nrm(ks[16], (L, D_FF, D), D_FF),
        "g_final": gain(ks[17], (D,)),
    }


def reference(x, c, w_ada, b_ada, g_ffn1, w1_gate, w1_up, w1_down, g_mix, w_in, w_pool,
              pool_scale, w_out, g_ffn2, w2_gate, w2_up, w2_down, g_final):
    h = x
    for l in range(DEPTH):
        mod = (jax.nn.silu(c) @ w_ada[l] + b_ada[l])[:, None, :]
        sh1, sc1, gt1, sh2, sc2, gt2, sh3, sc3, gt3 = jnp.split(mod, N_MOD, axis=-1)
        n = modulate(rmsnorm(h, g_ffn1[l]), sh1, sc1)
        h = h + 0.5 * gt1 * swiglu(n, w1_gate[l], w1_up[l], w1_down[l])
        n = modulate(rmsnorm(h, g_mix[l]), sh2, sc2)
        h = h + gt2 * hybrid_mixer(n, w_in[l], w_pool[l], pool_scale[l], w_out[l])
        n = modulate(rmsnorm(h, g_ffn2[l]), sh3, sc3)
        h = h + 0.5 * gt3 * swiglu(n, w2_gate[l], w2_up[l], w2_down[l])
    return rmsnorm(h, g_final)
```

```python
import functools

import jax
import jax.numpy as jnp
import numpy as np
from jax.experimental import pallas as pl
from jax.experimental.pallas import tpu as pltpu

D_MODEL = 1024
D_POOL = 512
D_ATTN = 512
POOL_WINDOWS = (2, 4, 8, 16)
POOL_GROUP_DIM = 128
HEAD_DIM = 64
N_HEADS = 8
WINDOW = 128
DILATIONS = (1, 4, 16)
D_FF = 2816
N_MOD = 9
EPS = 1e-6
NEG = -1e30

LANES = 128
PERM_ROWS = 256
O_EXT = D_ATTN + LANES
VMEM_LIMIT = 56 * 1024 * 1024

_f32 = jnp.float32
_bf16 = jnp.bfloat16


def _rms_mod(x, g, shift, scale):
    r = jax.lax.rsqrt(jnp.mean(x * x, axis=-1, keepdims=True) + EPS)
    return x * r * (g * (1.0 + scale)) + shift


def _resident(shape):
    nd = len(shape)
    return pl.BlockSpec(shape, lambda *_: (0,) * nd, pipeline_mode=pl.Buffered(1))


def _ada_kernel(c_ref, w_ref, b_ref, o_ref):
    c = c_ref[...]
    a = (c * (1.0 / (1.0 + jnp.exp(-c)))).astype(_bf16)
    o_ref[...] = jnp.dot(a, w_ref[...].astype(_bf16), preferred_element_type=_f32) + b_ref[...]


def _ada_mod(c, w_ada, b_ada):
    B, D = c.shape
    N = w_ada.shape[1]
    bn = 1536
    return pl.pallas_call(
        _ada_kernel,
        grid=(N // bn,),
        in_specs=[pl.BlockSpec((B, D), lambda j: (0, 0)),
                  pl.BlockSpec((D, bn), lambda j: (0, j)),
                  pl.BlockSpec((1, bn), lambda j: (0, j))],
        out_specs=pl.BlockSpec((B, bn), lambda j: (0, j)),
        out_shape=jax.ShapeDtypeStruct((B, N), _f32),
        compiler_params=pltpu.CompilerParams(dimension_semantics=("arbitrary",),
                                             vmem_limit_bytes=VMEM_LIMIT),
        name="ada_mod",
    )(c, w_ada, b_ada.reshape(1, N))


FF_CHUNK = 256


def _ffn_kernel(x_ref, mod_ref, g_ref, wgu_ref, wd_ref, gf_ref, o_ref, *, mod_row, final_norm):
    x = x_ref[...]
    shift = mod_ref[mod_row:mod_row + 1, :]
    scale = mod_ref[mod_row + 1:mod_row + 2, :]
    gate = mod_ref[mod_row + 2:mod_row + 3, :]
    n = _rms_mod(x, g_ref[...], shift, scale).astype(_bf16)
    acc = jnp.zeros(x.shape, _f32)
    for c in range(D_FF // FF_CHUNK):
        lo = c * FF_CHUNK
        g = jnp.dot(n, wgu_ref[:, lo:lo + FF_CHUNK], preferred_element_type=_f32)
        u = jnp.dot(n, wgu_ref[:, D_FF + lo:D_FF + lo + FF_CHUNK], preferred_element_type=_f32)
        a = (g * (1.0 / (1.0 + jnp.exp(-g))) * u).astype(_bf16)
        acc = acc + jnp.dot(a, wd_ref[lo:lo + FF_CHUNK, :], preferred_element_type=_f32)
    h = x + (0.5 * gate) * acc
    if final_norm:
        h = h * jax.lax.rsqrt(jnp.mean(h * h, axis=-1, keepdims=True) + EPS) * gf_ref[...]
    o_ref[...] = h


def _ffn(x, mod3, g, wgu, wd, g_final, *, mod_row, final_norm, tm=512):
    B, S, D = x.shape
    kern = functools.partial(_ffn_kernel, mod_row=mod_row, final_norm=final_norm)
    return pl.pallas_call(
        kern,
        grid=(B, S // tm),
        in_specs=[pl.BlockSpec((None, tm, D), lambda b, i: (b, i, 0)),
                  pl.BlockSpec((None, N_MOD, D), lambda b, i: (b, 0, 0)),
                  _resident((1, D)),
                  _resident(wgu.shape),
                  _resident(wd.shape),
                  _resident((1, D))],
        out_specs=pl.BlockSpec((None, tm, D), lambda b, i: (b, i, 0)),
        out_shape=jax.ShapeDtypeStruct((B, S, D), _f32),
        compiler_params=pltpu.CompilerParams(dimension_semantics=("arbitrary", "arbitrary"),
                                             vmem_limit_bytes=VMEM_LIMIT),
        name="ffn_final" if final_norm else "ffn",
    )(x, mod3, g.reshape(1, D), wgu, wd, g_final.reshape(1, D))


def _sort_matrix(dil):
    per = PERM_ROWS // dil
    i = np.arange(PERM_ROWS)
    src = (i % per) * dil + i // per
    p = np.zeros((PERM_ROWS, PERM_ROWS), np.float32)
    p[i, src] = 1.0
    return p


def _split3(x):
    hi = x.astype(_bf16)
    r1 = x - hi.astype(_f32)
    mid = r1.astype(_bf16)
    lo = (r1 - mid.astype(_f32)).astype(_bf16)
    return hi, mid, lo


def _inproj_kernel(h_ref, mod_ref, g_ref, w_ref, p4_ref, p16_ref,
                   u_ref, q1_ref, k1_ref, v1_ref, q4_ref, k4_ref, v4_ref, q16_ref, k16_ref, v16_ref):
    x = h_ref[...]
    tm = x.shape[0]
    n = _rms_mod(x, g_ref[...], mod_ref[3:4, :], mod_ref[4:5, :]).astype(_bf16)
    z = jnp.dot(n, w_ref[...], preferred_element_type=_f32)
    u_ref[...] = z[:, :D_POOL]
    q = (z[:, D_POOL:D_POOL + D_ATTN] * (HEAD_DIM ** -0.5)).astype(_bf16)
    k = z[:, D_POOL + D_ATTN:D_POOL + 2 * D_ATTN].astype(_bf16)
    v = z[:, D_POOL + 2 * D_ATTN:].astype(_bf16)
    q1_ref[...] = q
    k1_ref[...] = k
    v1_ref[...] = v
    qkv = jnp.concatenate([q, k, v], axis=1)
    for dil, p_ref, outs in ((4, p4_ref, (q4_ref, k4_ref, v4_ref)),
                             (16, p16_ref, (q16_ref, k16_ref, v16_ref))):
        per = PERM_ROWS // dil
        for c in range(tm // PERM_ROWS):
            s = jnp.dot(p_ref[...], qkv[c * PERM_ROWS:(c + 1) * PERM_ROWS, :],
                        preferred_element_type=_f32).astype(_bf16)
            for a, o_ref in enumerate(outs):
                for cls in range(dil):
                    o_ref[cls, c * per:(c + 1) * per, :] = s[cls * per:(cls + 1) * per,
                                                             a * D_ATTN:(a + 1) * D_ATTN]


def _in_proj(h, mod3, g, w_in, p4, p16, tm=512):
    B, S, D = h.shape
    nat = jax.ShapeDtypeStruct((B, S, D_ATTN), _bf16)
    nat_spec = pl.BlockSpec((None, tm, D_ATTN), lambda b, i: (b, i, 0))

    def cls_shape(dil):
        return jax.ShapeDtypeStruct((B, dil, S // dil, D_ATTN), _bf16)

    def cls_spec(dil):
        return pl.BlockSpec((None, dil, tm // dil, D_ATTN), lambda b, i: (b, 0, i, 0))

    return pl.pallas_call(
        _inproj_kernel,
        grid=(B, S // tm),
        in_specs=[pl.BlockSpec((None, tm, D), lambda b, i: (b, i, 0)),
                  pl.BlockSpec((None, N_MOD, D), lambda b, i: (b, 0, 0)),
                  _resident((1, D)),
                  _resident(w_in.shape),
                  _resident(p4.shape),
                  _resident(p16.shape)],
        out_specs=[pl.BlockSpec((None, tm, D_POOL), lambda b, i: (b, i, 0)),
                   nat_spec, nat_spec, nat_spec,
                   cls_spec(4), cls_spec(4), cls_spec(4),
                   cls_spec(16), cls_spec(16), cls_spec(16)],
        out_shape=[jax.ShapeDtypeStruct((B, S, D_POOL), _f32), nat, nat, nat,
                   cls_shape(4), cls_shape(4), cls_shape(4),
                   cls_shape(16), cls_shape(16), cls_shape(16)],
        compiler_params=pltpu.CompilerParams(dimension_semantics=("arbitrary", "arbitrary"),
                                             vmem_limit_bytes=VMEM_LIMIT),
        name="in_proj",
    )(h, mod3, g.reshape(1, D), w_in, p4, p16)


QB = 128


def _attn_kernel(q_ref, km_ref, kh_ref, vm_ref, vh_ref, o_ref, kbuf, vbuf, bias, *, dil):
    b, c, i = pl.program_id(0), pl.program_id(1), pl.program_id(2)
    tq = q_ref.shape[0]

    @pl.when((b == 0) & (c == 0) & (i == 0))
    def _():
        row = jax.lax.broadcasted_iota(jnp.int32, (QB, 2 * QB), 0)
        col = jax.lax.broadcasted_iota(jnp.int32, (QB, 2 * QB), 1)
        delta = row - col + QB
        valid = (delta >= 0) & (delta <= WINDOW)
        dist = (delta * dil).astype(_f32)
        for h in range(N_HEADS):
            slope = 2.0 ** (-8.0 * (h + 1) / N_HEADS)
            bias[h] = jnp.where(valid, -slope * dist, NEG)

    kbuf[0:QB, :] = kh_ref[...]
    kbuf[QB:, :] = km_ref[...]
    vbuf[0:QB, :] = vh_ref[...]
    vbuf[QB:, :] = vm_ref[...]

    lane = jax.lax.broadcasted_iota(jnp.int32, (QB, LANES), 1)
    low_half = lane < HEAD_DIM
    col = jax.lax.broadcasted_iota(jnp.int32, (1, 2 * QB), 1)
    no_prev = jnp.where((col < QB) & (i == 0), NEG, 0.0)

    for j in range(tq // QB):
        lse_tile = jnp.zeros((QB, LANES), _f32)
        for hp in range(N_HEADS // 2):
            cs = slice(hp * LANES, (hp + 1) * LANES)
            q2 = q_ref[j * QB:(j + 1) * QB, cs]
            k2 = kbuf[j * QB:(j + 2) * QB, cs]
            v2 = vbuf[j * QB:(j + 2) * QB, cs]
            halves = []
            for e in range(2):
                h = 2 * hp + e
                qm = jnp.where(low_half if e == 0 else ~low_half, q2, jnp.zeros_like(q2))
                s = jax.lax.dot_general(qm, k2, (((1,), (1,)), ((), ())), preferred_element_type=_f32)
                s = s + bias[h]
                if j == 0:
                    s = s + no_prev
                m = jnp.max(s, axis=-1, keepdims=True)
                p = jnp.exp(s - m)
                l = jnp.sum(p, axis=-1, keepdims=True)
                pv = jnp.dot(p.astype(_bf16), v2, preferred_element_type=_f32)
                halves.append(pv * (1.0 / l))
                lse_tile = jnp.where(lane == h, m + jnp.log(l), lse_tile)
            o_ref[j * QB:(j + 1) * QB, cs] = jnp.where(low_half, halves[0], halves[1])
        o_ref[j * QB:(j + 1) * QB, D_ATTN:] = lse_tile


def _attention(q, k, v, dil):
    B, C, L, _ = q.shape
    tq = min(L, 512)
    main = pl.BlockSpec((None, None, tq, D_ATTN), lambda b, c, i: (b, c, i, 0))
    halo = pl.BlockSpec((None, None, QB, D_ATTN),
                        lambda b, c, i: (b, c, jnp.maximum(i * (tq // QB) - 1, 0), 0))
    return pl.pallas_call(
        functools.partial(_attn_kernel, dil=dil),
        grid=(B, C, L // tq),
        in_specs=[main, main, halo, main, halo],
        out_specs=pl.BlockSpec((None, None, tq, O_EXT), lambda b, c, i: (b, c, i, 0)),
        out_shape=jax.ShapeDtypeStruct((B, C, L, O_EXT), _f32),
        scratch_shapes=[pltpu.VMEM((tq + QB, D_ATTN), _bf16),
                        pltpu.VMEM((tq + QB, D_ATTN), _bf16),
                        pltpu.VMEM((N_HEADS, QB, 2 * QB), _f32)],
        compiler_params=pltpu.CompilerParams(dimension_semantics=("arbitrary",) * 3,
                                             vmem_limit_bytes=VMEM_LIMIT),
        name=f"attn_d{dil}",
    )(q, k, k, v, v)


POOL_HALO = 16


def _unsort(o_ref, pt_ref, c, dil):
    per = PERM_ROWS // dil
    x = jnp.concatenate([o_ref[cls, c * per:(c + 1) * per, :] for cls in range(dil)], axis=0)
    pt = pt_ref[...]
    return sum(jnp.dot(pt, part, preferred_element_type=_f32) for part in _split3(x))


def _mixout_kernel(o1_ref, o4_ref, o16_ref, u_ref, uh_ref, h_ref, mod_ref, wp_ref, ps_ref, wo_ref,
                   p4t_ref, p16t_ref, out_ref, ubuf, ybuf):
    i = pl.program_id(1)
    tm = h_ref.shape[0]

    ubuf[0:POOL_HALO, :] = jnp.where(i == 0, 0.0, uh_ref[...])
    ubuf[POOL_HALO:, :] = u_ref[...]
    t = i * tm + jax.lax.broadcasted_iota(jnp.int32, (tm, 1), 0)
    for g, w in enumerate(POOL_WINDOWS):
        cs = slice(g * POOL_GROUP_DIM, (g + 1) * POOL_GROUP_DIM)
        tok = ubuf[POOL_HALO:, cs]
        win = tok
        for back in range(1, w):
            win = win + ubuf[POOL_HALO - back:POOL_HALO - back + tm, cs]
        count = jnp.minimum(t + 1, w).astype(_f32)
        pooled = (win / count - tok).astype(_bf16)
        y = jnp.dot(pooled, wp_ref[g], preferred_element_type=_f32) * ps_ref[:, cs]
        ybuf[:, cs] = y.astype(_bf16)

    lane = jax.lax.broadcasted_iota(jnp.int32, (1, LANES), 1)
    for c in range(tm // PERM_ROWS):
        rows = slice(c * PERM_ROWS, (c + 1) * PERM_ROWS)
        branches = (o1_ref[rows, :], _unsort(o4_ref, p4t_ref, c, 4), _unsort(o16_ref, p16t_ref, c, 16))
        lses = [jnp.where(lane < N_HEADS, br[:, D_ATTN:], NEG) for br in branches]
        top = jnp.maximum(jnp.maximum(lses[0], lses[1]), lses[2])
        wts = [jnp.exp(l - top) for l in lses]
        inv = 1.0 / (wts[0] + wts[1] + wts[2])
        wts = [w * inv for w in wts]
        for h in range(N_HEADS):
            cs = slice(h * HEAD_DIM, (h + 1) * HEAD_DIM)
            y = sum(wts[r][:, h:h + 1] * branches[r][:, cs] for r in range(3))
            ybuf[rows, D_POOL + h * HEAD_DIM:D_POOL + (h + 1) * HEAD_DIM] = y.astype(_bf16)

    y = jnp.dot(ybuf[...], wo_ref[...], preferred_element_type=_f32)
    out_ref[...] = h_ref[...] + mod_ref[5:6, :] * y


def _mix_out(o1, o4, o16, u, h, mod3, w_pool, pool_scale, w_out, p4t, p16t, tm=512):
    B, S, D = h.shape
    return pl.pallas_call(
        _mixout_kernel,
        grid=(B, S // tm),
        in_specs=[pl.BlockSpec((None, None, tm, O_EXT), lambda b, i: (b, 0, i, 0)),
                  pl.BlockSpec((None, 4, tm // 4, O_EXT), lambda b, i: (b, 0, i, 0)),
                  pl.BlockSpec((None, 16, tm // 16, O_EXT), lambda b, i: (b, 0, i, 0)),
                  pl.BlockSpec((None, tm, D_POOL), lambda b, i: (b, i, 0)),
                  pl.BlockSpec((None, POOL_HALO, D_POOL),
                               lambda b, i: (b, jnp.maximum(i * (tm // POOL_HALO) - 1, 0), 0)),
                  pl.BlockSpec((None, tm, D), lambda b, i: (b, i, 0)),
                  pl.BlockSpec((None, N_MOD, D), lambda b, i: (b, 0, 0)),
                  _resident(w_pool.shape),
                  _resident((1, D_POOL)),
                  _resident(w_out.shape),
                  _resident(p4t.shape),
                  _resident(p16t.shape)],
        out_specs=pl.BlockSpec((None, tm, D), lambda b, i: (b, i, 0)),
        out_shape=jax.ShapeDtypeStruct((B, S, D), _f32),
        scratch_shapes=[pltpu.VMEM((tm + POOL_HALO, D_POOL), _f32),
                        pltpu.VMEM((tm, D), _bf16)],
        compiler_params=pltpu.CompilerParams(dimension_semantics=("arbitrary", "arbitrary"),
                                             vmem_limit_bytes=VMEM_LIMIT),
        name="mix_out",
    )(o1, o4, o16, u, u, h, mod3, w_pool, pool_scale.reshape(1, D_POOL), w_out, p4t, p16t)


def kernel(x, c, w_ada, b_ada, g_ffn1, w1_gate, w1_up, w1_down, g_mix, w_in, w_pool, pool_scale, w_out,
           g_ffn2, w2_gate, w2_up, w2_down, g_final):
    B, S, D = x.shape
    depth = w_ada.shape[0]
    p4 = jnp.asarray(_sort_matrix(4), _bf16)
    p16 = jnp.asarray(_sort_matrix(16), _bf16)
    p4t = jnp.asarray(_sort_matrix(4).T, _bf16)
    p16t = jnp.asarray(_sort_matrix(16).T, _bf16)
    h = x
    for l in range(depth):
        last = l == depth - 1
        mod3 = _ada_mod(c, w_ada[l], b_ada[l]).reshape(B, N_MOD, D)
        wgu1 = jnp.concatenate([w1_gate[l], w1_up[l]], axis=1).astype(_bf16)
        wgu2 = jnp.concatenate([w2_gate[l], w2_up[l]], axis=1).astype(_bf16)
        h = _ffn(h, mod3, g_ffn1[l], wgu1, w1_down[l].astype(_bf16), g_final, mod_row=0, final_norm=False)
        u, q1, k1, v1, q4, k4, v4, q16, k16, v16 = _in_proj(h, mod3, g_mix[l], w_in[l].astype(_bf16), p4, p16)
        o1 = _attention(q1[:, None], k1[:, None], v1[:, None], 1)
        o4 = _attention(q4, k4, v4, 4)
        o16 = _attention(q16, k16, v16, 16)
        h = _mix_out(o1, o4, o16, u, h, mod3, w_pool[l].astype(_bf16), pool_scale[l], w_out[l].astype(_bf16),
                     p4t, p16t)
        h = _ffn(h, mod3, g_ffn2[l], wgu2, w2_down[l].astype(_bf16), g_final, mod_row=6, final_norm=last)
    if depth == 0:
        raise ValueError("depth must be positive")
    return h
```

```python
import functools
import math

import jax
import jax.numpy as jnp
import numpy as np
from jax.experimental import pallas as pl
from jax.experimental.pallas import tpu as pltpu

D_MODEL = 1024
D_POOL = 512
D_ATTN = 512
POOL_WINDOWS = (2, 4, 8, 16)
POOL_GROUP_DIM = 128
HEAD_DIM = 64
N_HEADS = 8
WINDOW = 128
D_FF = 2816
N_MOD = 9
EPS = 1e-6
NEG = -1e30
LOG2E = math.log2(math.e)

LANES = 128
PERM_ROWS = 256
VMEM_LIMIT = 56 * 1024 * 1024

TM_FFN = 512
TM_PROJ = 512
TM_MIX = 512
TQ_ATTN = 512
BN_ADA = 1536

_f32 = jnp.float32
_bf16 = jnp.bfloat16


def _rms_mod(x, g, shift, scale):
    r = jax.lax.rsqrt(jnp.mean(x * x, axis=-1, keepdims=True) + EPS)
    return x * r * (g * (1.0 + scale)) + shift


def _resident(shape):
    nd = len(shape)
    return pl.BlockSpec(shape, lambda *_: (0,) * nd, pipeline_mode=pl.Buffered(1))


def _params(n_axes):
    return pltpu.CompilerParams(dimension_semantics=("arbitrary",) * n_axes, vmem_limit_bytes=VMEM_LIMIT)


def _ada_kernel(c_ref, w_ref, b_ref, o_ref):
    c = c_ref[...]
    a = (c * (1.0 / (1.0 + jnp.exp(-c)))).astype(_bf16)
    o_ref[...] = jnp.dot(a, w_ref[...].astype(_bf16), preferred_element_type=_f32) + b_ref[...]


def _ada_mod(c, w_ada, b_ada):
    B, D = c.shape
    N = w_ada.shape[1]
    return pl.pallas_call(
        _ada_kernel,
        grid=(N // BN_ADA,),
        in_specs=[pl.BlockSpec((B, D), lambda j: (0, 0)),
                  pl.BlockSpec((D, BN_ADA), lambda j: (0, j)),
                  pl.BlockSpec((1, BN_ADA), lambda j: (0, j))],
        out_specs=pl.BlockSpec((B, BN_ADA), lambda j: (0, j)),
        out_shape=jax.ShapeDtypeStruct((B, N), _f32),
        compiler_params=_params(1),
        name="ada_mod",
    )(c, w_ada, b_ada.reshape(1, N))


FF_CHUNK = 256


def _ffn_kernel(x_ref, mod_ref, g_ref, wg_ref, wu_ref, wd_ref, gf_ref, o_ref, *, mod_row, final_norm):
    x = x_ref[...]
    shift = mod_ref[mod_row:mod_row + 1, :]
    scale = mod_ref[mod_row + 1:mod_row + 2, :]
    gate = mod_ref[mod_row + 2:mod_row + 3, :]
    n = _rms_mod(x, g_ref[...], shift, scale).astype(_bf16)
    acc = jnp.zeros(x.shape, _f32)
    for c in range(D_FF // FF_CHUNK):
        cs = slice(c * FF_CHUNK, (c + 1) * FF_CHUNK)
        g = jnp.dot(n, wg_ref[:, cs], preferred_element_type=_f32)
        u = jnp.dot(n, wu_ref[:, cs], preferred_element_type=_f32)
        a = (g * (1.0 / (1.0 + jnp.exp(-g))) * u).astype(_bf16)
        acc = acc + jnp.dot(a, wd_ref[cs, :], preferred_element_type=_f32)
    h = x + (0.5 * gate) * acc
    if final_norm:
        h = h * jax.lax.rsqrt(jnp.mean(h * h, axis=-1, keepdims=True) + EPS) * gf_ref[...]
    o_ref[...] = h


def _ffn(x, mod3, g, wg, wu, wd, g_final, *, mod_row, final_norm):
    B, S, D = x.shape
    tm = TM_FFN
    kern = functools.partial(_ffn_kernel, mod_row=mod_row, final_norm=final_norm)
    return pl.pallas_call(
        kern,
        grid=(B, S // tm),
        in_specs=[pl.BlockSpec((None, tm, D), lambda b, i: (b, i, 0)),
                  pl.BlockSpec((None, N_MOD, D), lambda b, i: (b, 0, 0)),
                  _resident((1, D)),
                  _resident(wg.shape),
                  _resident(wu.shape),
                  _resident(wd.shape),
                  _resident((1, D))],
        out_specs=pl.BlockSpec((None, tm, D), lambda b, i: (b, i, 0)),
        out_shape=jax.ShapeDtypeStruct((B, S, D), _f32),
        compiler_params=_params(2),
        name="ffn_final" if final_norm else "ffn",
    )(x, mod3, g.reshape(1, D), wg, wu, wd, g_final.reshape(1, D))


def _sort_matrix(dil):
    per = PERM_ROWS // dil
    i = np.arange(PERM_ROWS)
    src = (i % per) * dil + i // per
    p = np.zeros((PERM_ROWS, PERM_ROWS), np.float32)
    p[i, src] = 1.0
    return p


def _split3(x):
    hi = x.astype(_bf16)
    r1 = x - hi.astype(_f32)
    mid = r1.astype(_bf16)
    lo = (r1 - mid.astype(_f32)).astype(_bf16)
    return hi, mid, lo


def _inproj_kernel(h_ref, mod_ref, g_ref, w_ref, p4_ref, p16_ref,
                   u_ref, q1_ref, k1_ref, v1_ref, q4_ref, k4_ref, v4_ref, q16_ref, k16_ref, v16_ref):
    x = h_ref[...]
    tm = x.shape[0]
    n = _rms_mod(x, g_ref[...], mod_ref[3:4, :], mod_ref[4:5, :]).astype(_bf16)
    z = jnp.dot(n, w_ref[...], preferred_element_type=_f32)
    u_ref[...] = z[:, :D_POOL]
    q = (z[:, D_POOL:D_POOL + D_ATTN] * (HEAD_DIM ** -0.5 * LOG2E)).astype(_bf16)
    k = z[:, D_POOL + D_ATTN:D_POOL + 2 * D_ATTN].astype(_bf16)
    v = z[:, D_POOL + 2 * D_ATTN:].astype(_bf16)
    q1_ref[...] = q
    k1_ref[...] = k
    v1_ref[...] = v
    qkv = jnp.concatenate([q, k, v], axis=1)
    for dil, p_ref, outs in ((4, p4_ref, (q4_ref, k4_ref, v4_ref)),
                             (16, p16_ref, (q16_ref, k16_ref, v16_ref))):
        per = PERM_ROWS // dil
        for c in range(tm // PERM_ROWS):
            s = jnp.dot(p_ref[...], qkv[c * PERM_ROWS:(c + 1) * PERM_ROWS, :],
                        preferred_element_type=_f32).astype(_bf16)
            for a, o_ref in enumerate(outs):
                for cls in range(dil):
                    o_ref[cls, c * per:(c + 1) * per, :] = s[cls * per:(cls + 1) * per,
                                                             a * D_ATTN:(a + 1) * D_ATTN]


def _in_proj(h, mod3, g, w_in, p4, p16):
    B, S, D = h.shape
    tm = TM_PROJ
    nat = jax.ShapeDtypeStruct((B, S, D_ATTN), _bf16)
    nat_spec = pl.BlockSpec((None, tm, D_ATTN), lambda b, i: (b, i, 0))

    def cls_shape(dil):
        return jax.ShapeDtypeStruct((B, dil, S // dil, D_ATTN), _bf16)

    def cls_spec(dil):
        return pl.BlockSpec((None, dil, tm // dil, D_ATTN), lambda b, i: (b, 0, i, 0))

    return pl.pallas_call(
        _inproj_kernel,
        grid=(B, S // tm),
        in_specs=[pl.BlockSpec((None, tm, D), lambda b, i: (b, i, 0)),
                  pl.BlockSpec((None, N_MOD, D), lambda b, i: (b, 0, 0)),
                  _resident((1, D)),
                  _resident(w_in.shape),
                  _resident(p4.shape),
                  _resident(p16.shape)],
        out_specs=[pl.BlockSpec((None, tm, D_POOL), lambda b, i: (b, i, 0)),
                   nat_spec, nat_spec, nat_spec,
                   cls_spec(4), cls_spec(4), cls_spec(4),
                   cls_spec(16), cls_spec(16), cls_spec(16)],
        out_shape=[jax.ShapeDtypeStruct((B, S, D_POOL), _f32), nat, nat, nat,
                   cls_shape(4), cls_shape(4), cls_shape(4),
                   cls_shape(16), cls_shape(16), cls_shape(16)],
        compiler_params=_params(2),
        name="in_proj",
    )(h, mod3, g.reshape(1, D), w_in, p4, p16)


QB = 128


def _attn_kernel(q_ref, km_ref, kh_ref, vm_ref, vh_ref, o_ref, st_ref, kbuf, vbuf, bias, *, dil):
    b, c, i = pl.program_id(0), pl.program_id(1), pl.program_id(2)
    tq = q_ref.shape[0]

    @pl.when((b == 0) & (c == 0) & (i == 0))
    def _():
        row = jax.lax.broadcasted_iota(jnp.int32, (QB, 2 * QB), 0)
        col = jax.lax.broadcasted_iota(jnp.int32, (QB, 2 * QB), 1)
        delta = row - col + QB
        valid = (delta >= 0) & (delta <= WINDOW)
        dist = (delta * dil).astype(_f32)
        for h in range(N_HEADS):
            slope = 2.0 ** (-8.0 * (h + 1) / N_HEADS) * LOG2E
            bias[h] = jnp.where(valid, -slope * dist, NEG)

    kbuf[0:QB, :] = kh_ref[...]
    kbuf[QB:, :] = km_ref[...]
    vbuf[0:QB, :] = vh_ref[...]
    vbuf[QB:, :] = vm_ref[...]

    lane = jax.lax.broadcasted_iota(jnp.int32, (QB, LANES), 1)
    low_half = lane < HEAD_DIM
    col = jax.lax.broadcasted_iota(jnp.int32, (1, 2 * QB), 1)
    no_prev = jnp.where((col < QB) & (i == 0), NEG, 0.0)

    for j in range(tq // QB):
        rows = slice(j * QB, (j + 1) * QB)
        stats = jnp.zeros((QB, LANES), _f32)
        for hp in range(N_HEADS // 2):
            cs = slice(hp * LANES, (hp + 1) * LANES)
            q2 = q_ref[rows, cs]
            k2 = kbuf[j * QB:(j + 2) * QB, cs]
            v2 = vbuf[j * QB:(j + 2) * QB, cs]
            halves = []
            for e in range(2):
                h = 2 * hp + e
                qm = jnp.where(low_half if e == 0 else ~low_half, q2, jnp.zeros_like(q2))
                s = jax.lax.dot_general(qm, k2, (((1,), (1,)), ((), ())), preferred_element_type=_f32)
                s = s + bias[h]
                if j == 0:
                    s = s + no_prev
                m = jnp.max(s, axis=-1, keepdims=True)
                p = jnp.exp2(s - m)
                l = jnp.sum(p, axis=-1, keepdims=True)
                halves.append(jnp.dot(p.astype(_bf16), v2, preferred_element_type=_f32))
                stats = jnp.where(lane == h, m, stats)
                stats = jnp.where(lane == N_HEADS + h, l, stats)
            o_ref[rows, cs] = jnp.where(low_half, halves[0], halves[1]).astype(_bf16)
        st_ref[rows, :] = stats


def _attention(q, k, v, dil):
    B, C, L, _ = q.shape
    tq = min(L, TQ_ATTN)
    main = pl.BlockSpec((None, None, tq, D_ATTN), lambda b, c, i: (b, c, i, 0))
    halo = pl.BlockSpec((None, None, QB, D_ATTN),
                        lambda b, c, i: (b, c, jnp.maximum(i * (tq // QB) - 1, 0), 0))
    return pl.pallas_call(
        functools.partial(_attn_kernel, dil=dil),
        grid=(B, C, L // tq),
        in_specs=[main, main, halo, main, halo],
        out_specs=[main, pl.BlockSpec((None, None, tq, LANES), lambda b, c, i: (b, c, i, 0))],
        out_shape=[jax.ShapeDtypeStruct((B, C, L, D_ATTN), _bf16),
                   jax.ShapeDtypeStruct((B, C, L, LANES), _f32)],
        scratch_shapes=[pltpu.VMEM((tq + QB, D_ATTN), _bf16),
                        pltpu.VMEM((tq + QB, D_ATTN), _bf16),
                        pltpu.VMEM((N_HEADS, QB, 2 * QB), _f32)],
        compiler_params=_params(3),
        name=f"attn_d{dil}",
    )(q, k, k, v, v)


POOL_HALO = 16


def _gather_classes(ref, c, dil):
    per = PERM_ROWS // dil
    return jnp.concatenate([ref[cls, c * per:(c + 1) * per, :] for cls in range(dil)], axis=0)


def _unsort_num(o_ref, pt_ref, c, dil):
    return jnp.dot(pt_ref[...], _gather_classes(o_ref, c, dil), preferred_element_type=_f32)


def _unsort_stats(st_ref, pt_ref, c, dil):
    parts = jnp.concatenate(_split3(_gather_classes(st_ref, c, dil)), axis=1)
    y = jnp.dot(pt_ref[...], parts, preferred_element_type=_f32)
    return y[:, :LANES] + y[:, LANES:2 * LANES] + y[:, 2 * LANES:]


def _mixout_kernel(o1_ref, s1_ref, o4_ref, s4_ref, o16_ref, s16_ref, u_ref, uh_ref, h_ref, mod_ref,
                   wp_ref, ps_ref, wo_ref, p4t_ref, p16t_ref, out_ref, ubuf, ybuf):
    i = pl.program_id(1)
    tm = h_ref.shape[0]

    ubuf[0:POOL_HALO, :] = jnp.where(i == 0, 0.0, uh_ref[...])
    ubuf[POOL_HALO:, :] = u_ref[...]
    t = i * tm + jax.lax.broadcasted_iota(jnp.int32, (tm, 1), 0)
    for g, w in enumerate(POOL_WINDOWS):
        cs = slice(g * POOL_GROUP_DIM, (g + 1) * POOL_GROUP_DIM)
        tok = ubuf[POOL_HALO:, cs]
        win = tok
        for back in range(1, w):
            win = win + ubuf[POOL_HALO - back:POOL_HALO - back + tm, cs]
        count = jnp.minimum(t + 1, w).astype(_f32)
        pooled = (win / count - tok).astype(_bf16)
        y = jnp.dot(pooled, wp_ref[g], preferred_element_type=_f32) * ps_ref[:, cs]
        ybuf[:, cs] = y.astype(_bf16)

    low_half = jax.lax.broadcasted_iota(jnp.int32, (PERM_ROWS, LANES), 1) < HEAD_DIM
    for c in range(tm // PERM_ROWS):
        rows = slice(c * PERM_ROWS, (c + 1) * PERM_ROWS)
        nums = (o1_ref[rows, :].astype(_f32), _unsort_num(o4_ref, p4t_ref, c, 4),
                _unsort_num(o16_ref, p16t_ref, c, 16))
        stats = (s1_ref[rows, :], _unsort_stats(s4_ref, p4t_ref, c, 4), _unsort_stats(s16_ref, p16t_ref, c, 16))
        top = jnp.maximum(jnp.maximum(stats[0], stats[1]), stats[2])
        es = [jnp.exp2(s - top) for s in stats]
        dens = [pltpu.roll(s, LANES - N_HEADS, 1) for s in stats]
        inv = 1.0 / (es[0] * dens[0] + es[1] * dens[1] + es[2] * dens[2])
        wts = [e * inv for e in es]
        for hp in range(N_HEADS // 2):
            cs = slice(hp * LANES, (hp + 1) * LANES)
            y = None
            for r in range(3):
                w = jnp.where(low_half, wts[r][:, 2 * hp:2 * hp + 1], wts[r][:, 2 * hp + 1:2 * hp + 2])
                y = w * nums[r][:, cs] if y is None else y + w * nums[r][:, cs]
            ybuf[rows, D_POOL + hp * LANES:D_POOL + (hp + 1) * LANES] = y.astype(_bf16)

    y = jnp.dot(ybuf[...], wo_ref[...], preferred_element_type=_f32)
    out_ref[...] = h_ref[...] + mod_ref[5:6, :] * y


def _mix_out(a1, a4, a16, u, h, mod3, w_pool, pool_scale, w_out, p4t, p16t):
    B, S, D = h.shape
    tm = TM_MIX

    def branch_specs(dil, width):
        if dil == 1:
            return pl.BlockSpec((None, None, tm, width), lambda b, i: (b, 0, i, 0))
        return pl.BlockSpec((None, dil, tm // dil, width), lambda b, i: (b, 0, i, 0))

    return pl.pallas_call(
        _mixout_kernel,
        grid=(B, S // tm),
        in_specs=[branch_specs(1, D_ATTN), branch_specs(1, LANES),
                  branch_specs(4, D_ATTN), branch_specs(4, LANES),
                  branch_specs(16, D_ATTN), branch_specs(16, LANES),
                  pl.BlockSpec((None, tm, D_POOL), lambda b, i: (b, i, 0)),
                  pl.BlockSpec((None, POOL_HALO, D_POOL),
                               lambda b, i: (b, jnp.maximum(i * (tm // POOL_HALO) - 1, 0), 0)),
                  pl.BlockSpec((None, tm, D), lambda b, i: (b, i, 0)),
                  pl.BlockSpec((None, N_MOD, D), lambda b, i: (b, 0, 0)),
                  _resident(w_pool.shape),
                  _resident((1, D_POOL)),
                  _resident(w_out.shape),
                  _resident(p4t.shape),
                  _resident(p16t.shape)],
        out_specs=pl.BlockSpec((None, tm, D), lambda b, i: (b, i, 0)),
        out_shape=jax.ShapeDtypeStruct((B, S, D), _f32),
        scratch_shapes=[pltpu.VMEM((tm + POOL_HALO, D_POOL), _f32),
                        pltpu.VMEM((tm, D), _bf16)],
        compiler_params=_params(2),
        name="mix_out",
    )(*a1, *a4, *a16, u, u, h, mod3, w_pool, pool_scale.reshape(1, D_POOL), w_out, p4t, p16t)


def kernel(x, c, w_ada, b_ada, g_ffn1, w1_gate, w1_up, w1_down, g_mix, w_in, w_pool, pool_scale, w_out,
           g_ffn2, w2_gate, w2_up, w2_down, g_final):
    B, S, D = x.shape
    depth = w_ada.shape[0]
    p4 = jnp.asarray(_sort_matrix(4), _bf16)
    p16 = jnp.asarray(_sort_matrix(16), _bf16)
    p4t = jnp.asarray(_sort_matrix(4).T, _bf16)
    p16t = jnp.asarray(_sort_matrix(16).T, _bf16)
    h = x
    for l in range(depth):
        mod3 = _ada_mod(c, w_ada[l], b_ada[l]).reshape(B, N_MOD, D)
        h = _ffn(h, mod3, g_ffn1[l], w1_gate[l].astype(_bf16), w1_up[l].astype(_bf16), w1_down[l].astype(_bf16),
                 g_final, mod_row=0, final_norm=False)
        u, q1, k1, v1, q4, k4, v4, q16, k16, v16 = _in_proj(h, mod3, g_mix[l], w_in[l].astype(_bf16), p4, p16)
        a1 = _attention(q1[:, None], k1[:, None], v1[:, None], 1)
        a4 = _attention(q4, k4, v4, 4)
        a16 = _attention(q16, k16, v16, 16)
        h = _mix_out(a1, a4, a16, u, h, mod3, w_pool[l].astype(_bf16), pool_scale[l], w_out[l].astype(_bf16),
                     p4t, p16t)
        h = _ffn(h, mod3, g_ffn2[l], w2_gate[l].astype(_bf16), w2_up[l].astype(_bf16), w2_down[l].astype(_bf16),
                 g_final, mod_row=6, final_norm=(l == depth - 1))
    return h
```

```python
import functools
import math

import jax
import jax.numpy as jnp
import numpy as np
from jax.experimental import pallas as pl
from jax.experimental.pallas import tpu as pltpu

D_MODEL = 1024
D_POOL = 512
D_ATTN = 512
POOL_WINDOWS = (2, 4, 8, 16)
POOL_GROUP_DIM = 128
HEAD_DIM = 64
N_HEADS = 8
WINDOW = 128
D_FF = 2816
N_MOD = 9
EPS = 1e-6
NEG = -1e30
LOG2E = math.log2(math.e)

LANES = 128
PERM_ROWS = 256
VMEM_LIMIT = 56 * 1024 * 1024

TM_FFN = 1024
TM_PROJ = 512
TM_MIX = 1024
ATTN_ROWS = 1024
BN_ADA = 1536

_f32 = jnp.float32
_bf16 = jnp.bfloat16


def _rms_mod(x, g, shift, scale):
    r = jax.lax.rsqrt(jnp.mean(x * x, axis=-1, keepdims=True) + EPS)
    return x * r * (g * (1.0 + scale)) + shift


def _resident(shape):
    nd = len(shape)
    return pl.BlockSpec(shape, lambda *_: (0,) * nd, pipeline_mode=pl.Buffered(1))


def _params(n_axes):
    return pltpu.CompilerParams(dimension_semantics=("arbitrary",) * n_axes, vmem_limit_bytes=VMEM_LIMIT)


def _ada_kernel(c_ref, w_ref, b_ref, o_ref):
    c = c_ref[...]
    a = (c * (1.0 / (1.0 + jnp.exp(-c)))).astype(_bf16)
    o_ref[...] = jnp.dot(a, w_ref[...].astype(_bf16), preferred_element_type=_f32) + b_ref[...]


def _ada_mod(c, w_ada, b_ada):
    B, D = c.shape
    N = w_ada.shape[1]
    return pl.pallas_call(
        _ada_kernel,
        grid=(N // BN_ADA,),
        in_specs=[pl.BlockSpec((B, D), lambda j: (0, 0)),
                  pl.BlockSpec((D, BN_ADA), lambda j: (0, j)),
                  pl.BlockSpec((1, BN_ADA), lambda j: (0, j))],
        out_specs=pl.BlockSpec((B, BN_ADA), lambda j: (0, j)),
        out_shape=jax.ShapeDtypeStruct((B, N), _f32),
        compiler_params=_params(1),
        name="ada_mod",
    )(c, w_ada, b_ada.reshape(1, N))


FF_CHUNK = 256


def _ffn_kernel(x_ref, mod_ref, g_ref, wg_ref, wu_ref, wd_ref, gf_ref, o_ref, *, mod_row, final_norm):
    x = x_ref[...]
    shift = mod_ref[mod_row:mod_row + 1, :]
    scale = mod_ref[mod_row + 1:mod_row + 2, :]
    gate = mod_ref[mod_row + 2:mod_row + 3, :]
    n = _rms_mod(x, g_ref[...], shift, scale).astype(_bf16)
    acc = jnp.zeros(x.shape, _f32)
    for c in range(D_FF // FF_CHUNK):
        cs = slice(c * FF_CHUNK, (c + 1) * FF_CHUNK)
        g = jnp.dot(n, wg_ref[:, cs], preferred_element_type=_f32)
        u = jnp.dot(n, wu_ref[:, cs], preferred_element_type=_f32)
        a = (g * (1.0 / (1.0 + jnp.exp(-g))) * u).astype(_bf16)
        acc = acc + jnp.dot(a, wd_ref[cs, :], preferred_element_type=_f32)
    h = x + (0.5 * gate) * acc
    if final_norm:
        h = h * jax.lax.rsqrt(jnp.mean(h * h, axis=-1, keepdims=True) + EPS) * gf_ref[...]
    o_ref[...] = h


def _ffn(x, mod3, g, wg, wu, wd, g_final, *, mod_row, final_norm):
    B, S, D = x.shape
    tm = TM_FFN
    kern = functools.partial(_ffn_kernel, mod_row=mod_row, final_norm=final_norm)
    return pl.pallas_call(
        kern,
        grid=(B, S // tm),
        in_specs=[pl.BlockSpec((None, tm, D), lambda b, i: (b, i, 0)),
                  pl.BlockSpec((None, N_MOD, D), lambda b, i: (b, 0, 0)),
                  _resident((1, D)),
                  _resident(wg.shape),
                  _resident(wu.shape),
                  _resident(wd.shape),
                  _resident((1, D))],
        out_specs=pl.BlockSpec((None, tm, D), lambda b, i: (b, i, 0)),
        out_shape=jax.ShapeDtypeStruct((B, S, D), _f32),
        compiler_params=_params(2),
        name="ffn_final" if final_norm else "ffn",
    )(x, mod3, g.reshape(1, D), wg, wu, wd, g_final.reshape(1, D))


def _sort_matrix(dil):
    per = PERM_ROWS // dil
    i = np.arange(PERM_ROWS)
    src = (i % per) * dil + i // per
    p = np.zeros((PERM_ROWS, PERM_ROWS), np.float32)
    p[i, src] = 1.0
    return p


def _split3(x):
    hi = x.astype(_bf16)
    r1 = x - hi.astype(_f32)
    mid = r1.astype(_bf16)
    lo = (r1 - mid.astype(_f32)).astype(_bf16)
    return hi, mid, lo


def _inproj_kernel(h_ref, mod_ref, g_ref, w_ref, p4_ref, p16_ref,
                   u_ref, q1_ref, k1_ref, v1_ref, q4_ref, k4_ref, v4_ref, q16_ref, k16_ref, v16_ref):
    x = h_ref[...]
    tm = x.shape[0]
    n = _rms_mod(x, g_ref[...], mod_ref[3:4, :], mod_ref[4:5, :]).astype(_bf16)
    z = jnp.dot(n, w_ref[...], preferred_element_type=_f32)
    u_ref[...] = z[:, :D_POOL]
    q = (z[:, D_POOL:D_POOL + D_ATTN] * (HEAD_DIM ** -0.5 * LOG2E)).astype(_bf16)
    k = z[:, D_POOL + D_ATTN:D_POOL + 2 * D_ATTN].astype(_bf16)
    v = z[:, D_POOL + 2 * D_ATTN:].astype(_bf16)
    q1_ref[...] = q
    k1_ref[...] = k
    v1_ref[...] = v
    qkv = jnp.concatenate([q, k, v], axis=1)
    for dil, p_ref, outs in ((4, p4_ref, (q4_ref, k4_ref, v4_ref)),
                             (16, p16_ref, (q16_ref, k16_ref, v16_ref))):
        per = PERM_ROWS // dil
        for c in range(tm // PERM_ROWS):
            s = jnp.dot(p_ref[...], qkv[c * PERM_ROWS:(c + 1) * PERM_ROWS, :],
                        preferred_element_type=_f32).astype(_bf16)
            for a, o_ref in enumerate(outs):
                for cls in range(dil):
                    o_ref[cls, c * per:(c + 1) * per, :] = s[cls * per:(cls + 1) * per,
                                                             a * D_ATTN:(a + 1) * D_ATTN]


def _in_proj(h, mod3, g, w_in, p4, p16):
    B, S, D = h.shape
    tm = TM_PROJ
    nat = jax.ShapeDtypeStruct((B, S, D_ATTN), _bf16)
    nat_spec = pl.BlockSpec((None, tm, D_ATTN), lambda b, i: (b, i, 0))

    def cls_shape(dil):
        return jax.ShapeDtypeStruct((B, dil, S // dil, D_ATTN), _bf16)

    def cls_spec(dil):
        return pl.BlockSpec((None, dil, tm // dil, D_ATTN), lambda b, i: (b, 0, i, 0))

    return pl.pallas_call(
        _inproj_kernel,
        grid=(B, S // tm),
        in_specs=[pl.BlockSpec((None, tm, D), lambda b, i: (b, i, 0)),
                  pl.BlockSpec((None, N_MOD, D), lambda b, i: (b, 0, 0)),
                  _resident((1, D)),
                  _resident(w_in.shape),
                  _resident(p4.shape),
                  _resident(p16.shape)],
        out_specs=[pl.BlockSpec((None, tm, D_POOL), lambda b, i: (b, i, 0)),
                   nat_spec, nat_spec, nat_spec,
                   cls_spec(4), cls_spec(4), cls_spec(4),
                   cls_spec(16), cls_spec(16), cls_spec(16)],
        out_shape=[jax.ShapeDtypeStruct((B, S, D_POOL), _f32), nat, nat, nat,
                   cls_shape(4), cls_shape(4), cls_shape(4),
                   cls_shape(16), cls_shape(16), cls_shape(16)],
        compiler_params=_params(2),
        name="in_proj",
    )(h, mod3, g.reshape(1, D), w_in, p4, p16)


QB = 128


def _den_lane(h):
    return N_HEADS + h + (HEAD_DIM if h % 2 == 0 else 0)


def _attn_kernel(q_ref, km_ref, kh_ref, vm_ref, vh_ref, o_ref, st_ref, kbuf, vbuf, bias, *, dil):
    b, c, i = pl.program_id(0), pl.program_id(1), pl.program_id(2)
    n_cls, tq = q_ref.shape[0], q_ref.shape[1]

    @pl.when((b == 0) & (c == 0) & (i == 0))
    def _():
        row = jax.lax.broadcasted_iota(jnp.int32, (QB, 2 * QB), 0)
        col = jax.lax.broadcasted_iota(jnp.int32, (QB, 2 * QB), 1)
        delta = row - col + QB
        valid = (delta >= 0) & (delta <= WINDOW)
        dist = (delta * dil).astype(_f32)
        for h in range(N_HEADS):
            slope = 2.0 ** (-8.0 * (h + 1) / N_HEADS) * LOG2E
            bias[h] = jnp.where(valid, -slope * dist, NEG)

    kbuf[:, 0:QB, :] = kh_ref[...]
    kbuf[:, QB:, :] = km_ref[:, 0:QB, :]
    own_even = jax.lax.broadcasted_iota(jnp.int32, (1, 1, D_ATTN), 2) % LANES < HEAD_DIM
    for dst, src in ((slice(0, QB), vh_ref), (slice(QB, None), vm_ref)):
        v = src[...]
        ones = jnp.ones_like(v)
        vbuf[0, :, dst, :] = jnp.where(own_even, v, ones)
        vbuf[1, :, dst, :] = jnp.where(own_even, ones, v)

    lane = jax.lax.broadcasted_iota(jnp.int32, (QB, LANES), 1)
    low_half = lane < HEAD_DIM
    col = jax.lax.broadcasted_iota(jnp.int32, (1, 2 * QB), 1)
    no_prev = jnp.where((col < QB) & (i == 0), NEG, 0.0)

    for cls in range(n_cls):
        for j in range(tq // QB):
            rows = slice(j * QB, (j + 1) * QB)
            stats = jnp.zeros((QB, LANES), _f32)
            for hp in range(N_HEADS // 2):
                cs = slice(hp * LANES, (hp + 1) * LANES)
                q2 = q_ref[cls, rows, cs]
                k2 = kbuf[cls, :, cs] if j == 0 else km_ref[cls, (j - 1) * QB:(j + 1) * QB, cs]
                halves = []
                for e in range(2):
                    h = 2 * hp + e
                    qm = jnp.where(low_half if e == 0 else ~low_half, q2, jnp.zeros_like(q2))
                    s = jax.lax.dot_general(qm, k2, (((1,), (1,)), ((), ())), preferred_element_type=_f32)
                    s = s + bias[h]
                    if j == 0:
                        s = s + no_prev
                    m = jnp.max(s, axis=-1, keepdims=True)
                    p = jnp.exp2(s - m).astype(_bf16)
                    pv = jnp.dot(p, vbuf[e, cls, j * QB:(j + 2) * QB, cs], preferred_element_type=_f32)
                    halves.append(pv)
                    stats = jnp.where(lane == h, m, stats)
                    stats = jnp.where(lane == _den_lane(h), pv, stats)
                o_ref[cls, rows, cs] = jnp.where(low_half, halves[0], halves[1]).astype(_bf16)
            st_ref[cls, rows, :] = stats


def _attention(q, k, v, dil):
    B, C, L, _ = q.shape
    tq = min(L, ATTN_ROWS)
    n_cls = min(C, ATTN_ROWS // tq)

    def spec(rows, width, index):
        return pl.BlockSpec((None, n_cls, rows, width), index)

    def main_idx(b, c, i):
        return (b, c, i, 0)

    def halo_idx(b, c, i):
        return (b, c, jnp.maximum(i * (tq // QB) - 1, 0), 0)

    main, halo = spec(tq, D_ATTN, main_idx), spec(QB, D_ATTN, halo_idx)
    return pl.pallas_call(
        functools.partial(_attn_kernel, dil=dil),
        grid=(B, C // n_cls, L // tq),
        in_specs=[main, main, halo, main, halo],
        out_specs=[main, spec(tq, LANES, main_idx)],
        out_shape=[jax.ShapeDtypeStruct((B, C, L, D_ATTN), _bf16),
                   jax.ShapeDtypeStruct((B, C, L, LANES), _f32)],
        scratch_shapes=[pltpu.VMEM((n_cls, 2 * QB, D_ATTN), _bf16),
                        pltpu.VMEM((2, n_cls, tq + QB, D_ATTN), _bf16),
                        pltpu.VMEM((N_HEADS, QB, 2 * QB), _f32)],
        compiler_params=_params(3),
        name=f"attn_d{dil}",
    )(q, k, k, v, v)


POOL_HALO = 16


def _gather_classes(ref, c, dil):
    per = PERM_ROWS // dil
    return jnp.concatenate([ref[cls, c * per:(c + 1) * per, :] for cls in range(dil)], axis=0)


def _unsort_num(o_ref, pt_ref, c, dil):
    return jnp.dot(pt_ref[...], _gather_classes(o_ref, c, dil), preferred_element_type=_f32)


def _unsort_stats(st_ref, pt_ref, c, dil):
    parts = jnp.concatenate(_split3(_gather_classes(st_ref, c, dil)), axis=1)
    y = jnp.dot(pt_ref[...], parts, preferred_element_type=_f32)
    return y[:, :LANES] + y[:, LANES:2 * LANES] + y[:, 2 * LANES:]


def _pool_chunk(ubuf, wp_ref, ps_ref, c, t0):
    ext = PERM_ROWS + POOL_HALO
    t = t0 + jax.lax.broadcasted_iota(jnp.int32, (PERM_ROWS, 1), 0)
    ys = []
    for g, w in enumerate(POOL_WINDOWS):
        cs = slice(g * POOL_GROUP_DIM, (g + 1) * POOL_GROUP_DIM)
        x = ubuf[c * PERM_ROWS:c * PERM_ROWS + ext, cs]
        win = x
        back = 1
        while back < w:
            win = win + pltpu.roll(win, back, 0)
            back *= 2
        count = jnp.minimum(t + 1, w).astype(_f32)
        tok = x[POOL_HALO:, :]
        pooled = (win[POOL_HALO:, :] / count - tok).astype(_bf16)
        y = jnp.dot(pooled, wp_ref[g], preferred_element_type=_f32) * ps_ref[:, cs]
        ys.append(y.astype(_bf16))
    return jnp.concatenate(ys, axis=1)


def _merge_chunk(o1_ref, s1_ref, o4_ref, s4_ref, o16_ref, s16_ref, p4t_ref, p16t_ref, c):
    rows = slice(c * PERM_ROWS, (c + 1) * PERM_ROWS)
    nums = (o1_ref[rows, :].astype(_f32), _unsort_num(o4_ref, p4t_ref, c, 4),
            _unsort_num(o16_ref, p16t_ref, c, 16))
    stats = (s1_ref[rows, :], _unsort_stats(s4_ref, p4t_ref, c, 4), _unsort_stats(s16_ref, p16t_ref, c, 16))
    top = jnp.maximum(jnp.maximum(stats[0], stats[1]), stats[2])
    es = [jnp.exp2(s - top) for s in stats]
    even_head = jax.lax.broadcasted_iota(jnp.int32, (PERM_ROWS, LANES), 1) % 2 == 0
    dens = [jnp.where(even_head, pltpu.roll(s, LANES - _den_lane(0), 1), pltpu.roll(s, LANES - _den_lane(1) + 1, 1))
            for s in stats]
    inv = 1.0 / (es[0] * dens[0] + es[1] * dens[1] + es[2] * dens[2])
    wts = [e * inv for e in es]
    upper = (jax.lax.broadcasted_iota(jnp.int32, (PERM_ROWS, LANES), 1) >= HEAD_DIM).astype(jnp.int32)
    ys = []
    for hp in range(N_HEADS // 2):
        cs = slice(hp * LANES, (hp + 1) * LANES)
        src_lane = 2 * hp + upper
        y = None
        for r in range(3):
            w = jnp.take_along_axis(wts[r], src_lane, axis=1, mode="promise_in_bounds")
            y = w * nums[r][:, cs] if y is None else y + w * nums[r][:, cs]
        ys.append(y.astype(_bf16))
    return jnp.concatenate(ys, axis=1)


def _mixout_kernel(o1_ref, s1_ref, o4_ref, s4_ref, o16_ref, s16_ref, u_ref, uh_ref, h_ref, mod_ref,
                   wp_ref, ps_ref, wo_ref, p4t_ref, p16t_ref, out_ref, ubuf):
    i = pl.program_id(1)
    tm = h_ref.shape[0]
    ubuf[0:POOL_HALO, :] = jnp.where(i == 0, 0.0, uh_ref[...])
    ubuf[POOL_HALO:, :] = u_ref[...]
    gate = mod_ref[5:6, :]
    def mixed(c):
        y_pool = _pool_chunk(ubuf, wp_ref, ps_ref, c, i * tm + c * PERM_ROWS)
        y_attn = _merge_chunk(o1_ref, s1_ref, o4_ref, s4_ref, o16_ref, s16_ref, p4t_ref, p16t_ref, c)
        return jnp.concatenate([y_pool, y_attn], axis=1)

    n_chunks = tm // PERM_ROWS
    ready = mixed(0)
    for c in range(n_chunks):
        rows = slice(c * PERM_ROWS, (c + 1) * PERM_ROWS)
        cur = ready
        if c + 1 < n_chunks:
            ready = mixed(c + 1)
        y = jnp.dot(cur, wo_ref[...], preferred_element_type=_f32)
        out_ref[rows, :] = h_ref[rows, :] + gate * y


def _mix_out(a1, a4, a16, u, h, mod3, w_pool, pool_scale, w_out, p4t, p16t):
    B, S, D = h.shape
    tm = TM_MIX

    def branch_specs(dil, width):
        if dil == 1:
            return pl.BlockSpec((None, None, tm, width), lambda b, i: (b, 0, i, 0))
        return pl.BlockSpec((None, dil, tm // dil, width), lambda b, i: (b, 0, i, 0))

    return pl.pallas_call(
        _mixout_kernel,
        grid=(B, S // tm),
        in_specs=[branch_specs(1, D_ATTN), branch_specs(1, LANES),
                  branch_specs(4, D_ATTN), branch_specs(4, LANES),
                  branch_specs(16, D_ATTN), branch_specs(16, LANES),
                  pl.BlockSpec((None, tm, D_POOL), lambda b, i: (b, i, 0)),
                  pl.BlockSpec((None, POOL_HALO, D_POOL),
                               lambda b, i: (b, jnp.maximum(i * (tm // POOL_HALO) - 1, 0), 0)),
                  pl.BlockSpec((None, tm, D), lambda b, i: (b, i, 0)),
                  pl.BlockSpec((None, N_MOD, D), lambda b, i: (b, 0, 0)),
                  _resident(w_pool.shape),
                  _resident((1, D_POOL)),
                  _resident(w_out.shape),
                  _resident(p4t.shape),
                  _resident(p16t.shape)],
        out_specs=pl.BlockSpec((None, tm, D), lambda b, i: (b, i, 0)),
        out_shape=jax.ShapeDtypeStruct((B, S, D), _f32),
        scratch_shapes=[pltpu.VMEM((tm + POOL_HALO, D_POOL), _f32)],
        compiler_params=_params(2),
        name="mix_out",
    )(*a1, *a4, *a16, u, u, h, mod3, w_pool, pool_scale.reshape(1, D_POOL), w_out, p4t, p16t)


def kernel(x, c, w_ada, b_ada, g_ffn1, w1_gate, w1_up, w1_down, g_mix, w_in, w_pool, pool_scale, w_out,
           g_ffn2, w2_gate, w2_up, w2_down, g_final):
    B, S, D = x.shape
    depth = w_ada.shape[0]
    p4 = jnp.asarray(_sort_matrix(4), _bf16)
    p16 = jnp.asarray(_sort_matrix(16), _bf16)
    p4t = jnp.asarray(_sort_matrix(4).T, _bf16)
    p16t = jnp.asarray(_sort_matrix(16).T, _bf16)
    h = x
    for l in range(depth):
        mod3 = _ada_mod(c, w_ada[l], b_ada[l]).reshape(B, N_MOD, D)
        h = _ffn(h, mod3, g_ffn1[l], w1_gate[l].astype(_bf16), w1_up[l].astype(_bf16), w1_down[l].astype(_bf16),
                 g_final, mod_row=0, final_norm=False)
        u, q1, k1, v1, q4, k4, v4, q16, k16, v16 = _in_proj(h, mod3, g_mix[l], w_in[l].astype(_bf16), p4, p16)
        a1 = _attention(q1[:, None], k1[:, None], v1[:, None], 1)
        a4 = _attention(q4, k4, v4, 4)
        a16 = _attention(q16, k16, v16, 16)
        h = _mix_out(a1, a4, a16, u, h, mod3, w_pool[l].astype(_bf16), pool_scale[l], w_out[l].astype(_bf16),
                     p4t, p16t)
        h = _ffn(h, mod3, g_ffn2[l], w2_gate[l].astype(_bf16), w2_up[l].astype(_bf16), w2_down[l].astype(_bf16),
                 g_final, mod_row=6, final_norm=(l == depth - 1))
    return h
```

```python
import functools
import math

import jax
import jax.numpy as jnp
import numpy as np
from jax.experimental import pallas as pl
from jax.experimental.pallas import tpu as pltpu

D_MODEL = 1024
D_POOL = 512
D_ATTN = 512
POOL_WINDOWS = (2, 4, 8, 16)
POOL_GROUP_DIM = 128
N_POOL_GROUPS = len(POOL_WINDOWS)
HEAD_DIM = 64
N_HEADS = 8
WINDOW = 128
D_FF = 2816
N_MOD = 9
EPS = 1e-6
NEG = -1e30
LOG2E = math.log2(math.e)

LANES = 128
PERM_ROWS = 256
VMEM_LIMIT = 56 * 1024 * 1024

TM_FFN = 1024
TM_PROJ = 512
TM_MIX = 1024
ATTN_ROWS = 2048
BN_ADA = 1536

_f32 = jnp.float32
_bf16 = jnp.bfloat16


def _rms_mod(x, g, shift, scale):
    r = jax.lax.rsqrt(jnp.mean(x * x, axis=-1, keepdims=True) + EPS)
    return x * r * (g * (1.0 + scale)) + shift


def _resident(shape):
    nd = len(shape)
    return pl.BlockSpec(shape, lambda *_: (0,) * nd, pipeline_mode=pl.Buffered(1))


def _params(n_axes):
    return pltpu.CompilerParams(dimension_semantics=("arbitrary",) * n_axes, vmem_limit_bytes=VMEM_LIMIT)


def _ada_kernel(c_ref, w_ref, b_ref, o_ref):
    c = c_ref[...]
    a = (c * (1.0 / (1.0 + jnp.exp(-c)))).astype(_bf16)
    o_ref[...] = jnp.dot(a, w_ref[...].astype(_bf16), preferred_element_type=_f32) + b_ref[...]


def _ada_mod(c, w_ada, b_ada):
    B, D = c.shape
    N = w_ada.shape[1]
    return pl.pallas_call(
        _ada_kernel,
        grid=(N // BN_ADA,),
        in_specs=[pl.BlockSpec((B, D), lambda j: (0, 0)),
                  pl.BlockSpec((D, BN_ADA), lambda j: (0, j)),
                  pl.BlockSpec((1, BN_ADA), lambda j: (0, j))],
        out_specs=pl.BlockSpec((B, BN_ADA), lambda j: (0, j)),
        out_shape=jax.ShapeDtypeStruct((B, N), _f32),
        compiler_params=_params(1),
        name="ada_mod",
    )(c, w_ada, b_ada.reshape(1, N))


FF_CHUNK = 256


N_FFN_IN = 7


def _ffn_kernel(*refs, mod_row, final_norm):
    x_ref, mod_ref, g_ref, wg_ref, wu_ref, wd_ref, gf_ref = refs[:N_FFN_IN]
    n_cast = (len(refs) - N_FFN_IN - 1) // 2
    cast_in = refs[N_FFN_IN:N_FFN_IN + n_cast]
    o_ref = refs[N_FFN_IN + n_cast]
    cast_out = refs[N_FFN_IN + n_cast + 1:]
    shift = mod_ref[mod_row:mod_row + 1, :]
    scale = mod_ref[mod_row + 1:mod_row + 2, :]
    gate = mod_ref[mod_row + 2:mod_row + 3, :]
    x = x_ref[...]
    n = _rms_mod(x, g_ref[...], shift, scale).astype(_bf16)
    for src, dst in zip(cast_in, cast_out):
        dst[...] = src[...].astype(_bf16)
    acc = jnp.zeros(x.shape, _f32)
    for c in range(D_FF // FF_CHUNK):
        cs = slice(c * FF_CHUNK, (c + 1) * FF_CHUNK)
        g = jnp.dot(n, wg_ref[:, cs], preferred_element_type=_f32)
        u = jnp.dot(n, wu_ref[:, cs], preferred_element_type=_f32)
        a = (g * (1.0 / (1.0 + jnp.exp(-g))) * u).astype(_bf16)
        acc = acc + jnp.dot(a, wd_ref[cs, :], preferred_element_type=_f32)
    h = x + (0.5 * gate) * acc
    if final_norm:
        h = h * jax.lax.rsqrt(jnp.mean(h * h, axis=-1, keepdims=True) + EPS) * gf_ref[...]
    o_ref[...] = h


def _ffn(x, mod3, g, wg, wu, wd, g_final, *, mod_row, final_norm, narrow=()):
    B, S, D = x.shape
    tm = TM_FFN
    steps = B * (S // tm)
    per_b = S // tm

    def cast_spec(w):
        return pl.BlockSpec((w.shape[0] // steps, w.shape[1]), lambda b, i: (b * per_b + i, 0))

    kern = functools.partial(_ffn_kernel, mod_row=mod_row, final_norm=final_norm)
    outs = pl.pallas_call(
        kern,
        grid=(B, per_b),
        in_specs=[pl.BlockSpec((None, tm, D), lambda b, i: (b, i, 0)),
                  pl.BlockSpec((None, N_MOD, D), lambda b, i: (b, 0, 0)),
                  _resident((1, D)),
                  _resident(wg.shape),
                  _resident(wu.shape),
                  _resident(wd.shape),
                  _resident((1, D))] + [cast_spec(w) for w in narrow],
        out_specs=[pl.BlockSpec((None, tm, D), lambda b, i: (b, i, 0))] + [cast_spec(w) for w in narrow],
        out_shape=[jax.ShapeDtypeStruct((B, S, D), _f32)]
                  + [jax.ShapeDtypeStruct(w.shape, _bf16) for w in narrow],
        compiler_params=_params(2),
        name="ffn_final" if final_norm else "ffn",
    )(x, mod3, g.reshape(1, D), wg, wu, wd, g_final.reshape(1, D), *narrow)
    return outs[0], outs[1:]


def _sort_matrix(dil):
    per = PERM_ROWS // dil
    i = np.arange(PERM_ROWS)
    src = (i % per) * dil + i // per
    p = np.zeros((PERM_ROWS, PERM_ROWS), np.float32)
    p[i, src] = 1.0
    return p


def _split3(x):
    hi = x.astype(_bf16)
    r1 = x - hi.astype(_f32)
    mid = r1.astype(_bf16)
    lo = (r1 - mid.astype(_f32)).astype(_bf16)
    return hi, mid, lo


def _inproj_kernel(h_ref, mod_ref, g_ref, w_ref, p4_ref, p16_ref,
                   u_ref, q1_ref, k1_ref, v1_ref, q4_ref, k4_ref, v4_ref, q16_ref, k16_ref, v16_ref):
    x = h_ref[...]
    tm = x.shape[0]
    n = _rms_mod(x, g_ref[...], mod_ref[3:4, :], mod_ref[4:5, :]).astype(_bf16)
    z = jnp.dot(n, w_ref[...], preferred_element_type=_f32)
    u_ref[...] = z[:, :D_POOL]
    q = (z[:, D_POOL:D_POOL + D_ATTN] * (HEAD_DIM ** -0.5 * LOG2E)).astype(_bf16)
    k = z[:, D_POOL + D_ATTN:D_POOL + 2 * D_ATTN].astype(_bf16)
    v = z[:, D_POOL + 2 * D_ATTN:].astype(_bf16)
    q1_ref[...] = q
    k1_ref[...] = k
    v1_ref[...] = v
    qkv = jnp.concatenate([q, k, v], axis=1)
    for dil, p_ref, outs in ((4, p4_ref, (q4_ref, k4_ref, v4_ref)),
                             (16, p16_ref, (q16_ref, k16_ref, v16_ref))):
        per = PERM_ROWS // dil
        for c in range(tm // PERM_ROWS):
            s = jnp.dot(p_ref[...], qkv[c * PERM_ROWS:(c + 1) * PERM_ROWS, :],
                        preferred_element_type=_f32).astype(_bf16)
            for a, o_ref in enumerate(outs):
                for cls in range(dil):
                    o_ref[cls, c * per:(c + 1) * per, :] = s[cls * per:(cls + 1) * per,
                                                             a * D_ATTN:(a + 1) * D_ATTN]


def _in_proj(h, mod3, g, w_in, p4, p16):
    B, S, D = h.shape
    tm = TM_PROJ
    nat = jax.ShapeDtypeStruct((B, S, D_ATTN), _bf16)
    nat_spec = pl.BlockSpec((None, tm, D_ATTN), lambda b, i: (b, i, 0))

    def cls_shape(dil):
        return jax.ShapeDtypeStruct((B, dil, S // dil, D_ATTN), _bf16)

    def cls_spec(dil):
        return pl.BlockSpec((None, dil, tm // dil, D_ATTN), lambda b, i: (b, 0, i, 0))

    return pl.pallas_call(
        _inproj_kernel,
        grid=(B, S // tm),
        in_specs=[pl.BlockSpec((None, tm, D), lambda b, i: (b, i, 0)),
                  pl.BlockSpec((None, N_MOD, D), lambda b, i: (b, 0, 0)),
                  _resident((1, D)),
                  _resident(w_in.shape),
                  _resident(p4.shape),
                  _resident(p16.shape)],
        out_specs=[pl.BlockSpec((None, tm, D_POOL), lambda b, i: (b, i, 0)),
                   nat_spec, nat_spec, nat_spec,
                   cls_spec(4), cls_spec(4), cls_spec(4),
                   cls_spec(16), cls_spec(16), cls_spec(16)],
        out_shape=[jax.ShapeDtypeStruct((B, S, D_POOL), _f32), nat, nat, nat,
                   cls_shape(4), cls_shape(4), cls_shape(4),
                   cls_shape(16), cls_shape(16), cls_shape(16)],
        compiler_params=_params(2),
        name="in_proj",
    )(h, mod3, g.reshape(1, D), w_in, p4, p16)


QB = 128


def _den_lane(h):
    return N_HEADS + h + (HEAD_DIM if h % 2 == 0 else 0)


def _attn_kernel(q_ref, km_ref, kh_ref, vm_ref, vh_ref, o_ref, st_ref, kbuf, vbuf, bias, *, dil):
    b, c, i = pl.program_id(0), pl.program_id(1), pl.program_id(2)
    n_cls, tq = q_ref.shape[0], q_ref.shape[1]

    @pl.when((b == 0) & (c == 0) & (i == 0))
    def _():
        row = jax.lax.broadcasted_iota(jnp.int32, (QB, 2 * QB), 0)
        col = jax.lax.broadcasted_iota(jnp.int32, (QB, 2 * QB), 1)
        delta = row - col + QB
        valid = (delta >= 0) & (delta <= WINDOW)
        dist = (delta * dil).astype(_f32)
        for h in range(N_HEADS):
            slope = 2.0 ** (-8.0 * (h + 1) / N_HEADS) * LOG2E
            bias[h] = jnp.where(valid, -slope * dist, NEG)

    kbuf[:, 0:QB, :] = kh_ref[...]
    kbuf[:, QB:, :] = km_ref[:, 0:QB, :]
    own_even = jax.lax.broadcasted_iota(jnp.int32, (1, 1, D_ATTN), 2) % LANES < HEAD_DIM
    for dst, src in ((slice(0, QB), vh_ref), (slice(QB, None), vm_ref)):
        v = src[...]
        ones = jnp.ones_like(v)
        vbuf[0, :, dst, :] = jnp.where(own_even, v, ones)
        vbuf[1, :, dst, :] = jnp.where(own_even, ones, v)

    lane = jax.lax.broadcasted_iota(jnp.int32, (QB, LANES), 1)
    low_half = lane < HEAD_DIM
    col = jax.lax.broadcasted_iota(jnp.int32, (1, 2 * QB), 1)
    no_prev = jnp.where((col < QB) & (i == 0), NEG, 0.0)

    for cls in range(n_cls):
        for j in range(tq // QB):
            rows = slice(j * QB, (j + 1) * QB)
            stats = jnp.zeros((QB, LANES), _f32)
            for hp in range(N_HEADS // 2):
                cs = slice(hp * LANES, (hp + 1) * LANES)
                q2 = q_ref[cls, rows, cs]
                k2 = kbuf[cls, :, cs] if j == 0 else km_ref[cls, (j - 1) * QB:(j + 1) * QB, cs]
                halves = []
                for e in range(2):
                    h = 2 * hp + e
                    qm = jnp.where(low_half if e == 0 else ~low_half, q2, jnp.zeros_like(q2))
                    s = jax.lax.dot_general(qm, k2, (((1,), (1,)), ((), ())), preferred_element_type=_f32)
                    s = s + bias[h]
                    if j == 0:
                        s = s + no_prev
                    m = jnp.max(s, axis=-1, keepdims=True)
                    p = jnp.exp2(s - m).astype(_bf16)
                    pv = jnp.dot(p, vbuf[e, cls, j * QB:(j + 2) * QB, cs], preferred_element_type=_f32)
                    halves.append(pv)
                    stats = jnp.where(lane == h, m, stats)
                    stats = jnp.where(lane == _den_lane(h), pv, stats)
                o_ref[cls, rows, cs] = jnp.where(low_half, halves[0], halves[1]).astype(_bf16)
            st_ref[cls, rows, :] = stats


def _attention(q, k, v, dil):
    B, C, L, _ = q.shape
    tq = min(L, ATTN_ROWS)
    n_cls = min(C, ATTN_ROWS // tq)

    def spec(rows, width, index):
        return pl.BlockSpec((None, n_cls, rows, width), index)

    def main_idx(b, c, i):
        return (b, c, i, 0)

    def halo_idx(b, c, i):
        return (b, c, jnp.maximum(i * (tq // QB) - 1, 0), 0)

    main, halo = spec(tq, D_ATTN, main_idx), spec(QB, D_ATTN, halo_idx)
    return pl.pallas_call(
        functools.partial(_attn_kernel, dil=dil),
        grid=(B, C // n_cls, L // tq),
        in_specs=[main, main, halo, main, halo],
        out_specs=[main, spec(tq, LANES, main_idx)],
        out_shape=[jax.ShapeDtypeStruct((B, C, L, D_ATTN), _bf16),
                   jax.ShapeDtypeStruct((B, C, L, LANES), _f32)],
        scratch_shapes=[pltpu.VMEM((n_cls, 2 * QB, D_ATTN), _bf16),
                        pltpu.VMEM((2, n_cls, tq + QB, D_ATTN), _bf16),
                        pltpu.VMEM((N_HEADS, QB, 2 * QB), _f32)],
        compiler_params=_params(3),
        name=f"attn_d{dil}",
    )(q, k, k, v, v)


POOL_HALO = 16


def _gather_classes(ref, c, dil):
    per = PERM_ROWS // dil
    return jnp.concatenate([ref[cls, c * per:(c + 1) * per, :] for cls in range(dil)], axis=0)


def _unsort_num(o_ref, pt_ref, c, dil):
    return jnp.dot(pt_ref[...], _gather_classes(o_ref, c, dil), preferred_element_type=_f32)


def _unsort_stats(st_ref, pt_ref, c, dil):
    parts = jnp.concatenate(_split3(_gather_classes(st_ref, c, dil)), axis=1)
    y = jnp.dot(pt_ref[...], parts, preferred_element_type=_f32)
    return y[:, :LANES] + y[:, LANES:2 * LANES] + y[:, 2 * LANES:]


def _pool_chunk(ubuf, wp_ref, ps_ref, c, t0):
    ext = PERM_ROWS + POOL_HALO
    t = t0 + jax.lax.broadcasted_iota(jnp.int32, (PERM_ROWS, 1), 0)
    ys = []
    for g, w in enumerate(POOL_WINDOWS):
        cs = slice(g * POOL_GROUP_DIM, (g + 1) * POOL_GROUP_DIM)
        x = ubuf[c * PERM_ROWS:c * PERM_ROWS + ext, cs]
        win = x
        back = 1
        while back < w:
            win = win + pltpu.roll(win, back, 0)
            back *= 2
        count = jnp.minimum(t + 1, w).astype(_f32)
        tok = x[POOL_HALO:, :]
        pooled = (win[POOL_HALO:, :] / count - tok).astype(_bf16)
        y = jnp.dot(pooled, wp_ref[g], preferred_element_type=_f32) * ps_ref[:, cs]
        ys.append(y.astype(_bf16))
    return jnp.concatenate(ys, axis=1)


def _merge_chunk(o1_ref, s1_ref, o4_ref, s4_ref, o16_ref, s16_ref, p4t_ref, p16t_ref, c):
    rows = slice(c * PERM_ROWS, (c + 1) * PERM_ROWS)
    nums = (o1_ref[rows, :].astype(_f32), _unsort_num(o4_ref, p4t_ref, c, 4),
            _unsort_num(o16_ref, p16t_ref, c, 16))
    stats = (s1_ref[rows, :], _unsort_stats(s4_ref, p4t_ref, c, 4), _unsort_stats(s16_ref, p16t_ref, c, 16))
    top = jnp.maximum(jnp.maximum(stats[0], stats[1]), stats[2])
    es = [jnp.exp2(s - top) for s in stats]
    even_head = jax.lax.broadcasted_iota(jnp.int32, (PERM_ROWS, LANES), 1) % 2 == 0
    dens = [jnp.where(even_head, pltpu.roll(s, LANES - _den_lane(0), 1), pltpu.roll(s, LANES - _den_lane(1) + 1, 1))
            for s in stats]
    inv = 1.0 / (es[0] * dens[0] + es[1] * dens[1] + es[2] * dens[2])
    wts = [e * inv for e in es]
    upper = (jax.lax.broadcasted_iota(jnp.int32, (PERM_ROWS, LANES), 1) >= HEAD_DIM).astype(jnp.int32)
    ys = []
    for hp in range(N_HEADS // 2):
        cs = slice(hp * LANES, (hp + 1) * LANES)
        src_lane = 2 * hp + upper
        y = None
        for r in range(3):
            w = jnp.take_along_axis(wts[r], src_lane, axis=1, mode="promise_in_bounds")
            y = w * nums[r][:, cs] if y is None else y + w * nums[r][:, cs]
        ys.append(y.astype(_bf16))
    return jnp.concatenate(ys, axis=1)


def _mixout_kernel(o1_ref, s1_ref, o4_ref, s4_ref, o16_ref, s16_ref, u_ref, uh_ref, h_ref, mod_ref,
                   wp_ref, ps_ref, wo_ref, p4t_ref, p16t_ref, out_ref, ubuf):
    i = pl.program_id(1)
    tm = h_ref.shape[0]
    ubuf[0:POOL_HALO, :] = jnp.where(i == 0, 0.0, uh_ref[...])
    ubuf[POOL_HALO:, :] = u_ref[...]
    gate = mod_ref[5:6, :]
    def mixed(c):
        y_pool = _pool_chunk(ubuf, wp_ref, ps_ref, c, i * tm + c * PERM_ROWS)
        y_attn = _merge_chunk(o1_ref, s1_ref, o4_ref, s4_ref, o16_ref, s16_ref, p4t_ref, p16t_ref, c)
        return jnp.concatenate([y_pool, y_attn], axis=1)

    n_chunks = tm // PERM_ROWS
    ready = mixed(0)
    for c in range(n_chunks):
        rows = slice(c * PERM_ROWS, (c + 1) * PERM_ROWS)
        cur = ready
        if c + 1 < n_chunks:
            ready = mixed(c + 1)
        y = jnp.dot(cur, wo_ref[...], preferred_element_type=_f32)
        out_ref[rows, :] = h_ref[rows, :] + gate * y


def _mix_out(a1, a4, a16, u, h, mod3, w_pool, pool_scale, w_out, p4t, p16t):
    B, S, D = h.shape
    tm = TM_MIX

    def branch_specs(dil, width):
        if dil == 1:
            return pl.BlockSpec((None, None, tm, width), lambda b, i: (b, 0, i, 0))
        return pl.BlockSpec((None, dil, tm // dil, width), lambda b, i: (b, 0, i, 0))

    return pl.pallas_call(
        _mixout_kernel,
        grid=(B, S // tm),
        in_specs=[branch_specs(1, D_ATTN), branch_specs(1, LANES),
                  branch_specs(4, D_ATTN), branch_specs(4, LANES),
                  branch_specs(16, D_ATTN), branch_specs(16, LANES),
                  pl.BlockSpec((None, tm, D_POOL), lambda b, i: (b, i, 0)),
                  pl.BlockSpec((None, POOL_HALO, D_POOL),
                               lambda b, i: (b, jnp.maximum(i * (tm // POOL_HALO) - 1, 0), 0)),
                  pl.BlockSpec((None, tm, D), lambda b, i: (b, i, 0)),
                  pl.BlockSpec((None, N_MOD, D), lambda b, i: (b, 0, 0)),
                  _resident(w_pool.shape),
                  _resident((1, D_POOL)),
                  _resident(w_out.shape),
                  _resident(p4t.shape),
                  _resident(p16t.shape)],
        out_specs=pl.BlockSpec((None, tm, D), lambda b, i: (b, i, 0)),
        out_shape=jax.ShapeDtypeStruct((B, S, D), _f32),
        scratch_shapes=[pltpu.VMEM((tm + POOL_HALO, D_POOL), _f32)],
        compiler_params=_params(2),
        name="mix_out",
    )(*a1, *a4, *a16, u, u, h, mod3, w_pool, pool_scale.reshape(1, D_POOL), w_out, p4t, p16t)


def kernel(x, c, w_ada, b_ada, g_ffn1, w1_gate, w1_up, w1_down, g_mix, w_in, w_pool, pool_scale, w_out,
           g_ffn2, w2_gate, w2_up, w2_down, g_final):
    B, S, D = x.shape
    depth = w_ada.shape[0]
    p4 = jnp.asarray(_sort_matrix(4), _bf16)
    p16 = jnp.asarray(_sort_matrix(16), _bf16)
    p4t = jnp.asarray(_sort_matrix(4).T, _bf16)
    p16t = jnp.asarray(_sort_matrix(16).T, _bf16)
    h = x
    for l in range(depth):
        mod3 = _ada_mod(c, w_ada[l], b_ada[l]).reshape(B, N_MOD, D)
        pool_flat = w_pool[l].reshape(N_POOL_GROUPS * POOL_GROUP_DIM, POOL_GROUP_DIM)
        h, (wg2, wu2, wd2, w_in_b, w_out_b, w_pool_b) = _ffn(
            h, mod3, g_ffn1[l], w1_gate[l].astype(_bf16), w1_up[l].astype(_bf16), w1_down[l].astype(_bf16),
            g_final, mod_row=0, final_norm=False,
            narrow=(w2_gate[l], w2_up[l], w2_down[l], w_in[l], w_out[l], pool_flat))
        u, q1, k1, v1, q4, k4, v4, q16, k16, v16 = _in_proj(h, mod3, g_mix[l], w_in_b, p4, p16)
        a1 = _attention(q1[:, None], k1[:, None], v1[:, None], 1)
        a4 = _attention(q4, k4, v4, 4)
        a16 = _attention(q16, k16, v16, 16)
        h = _mix_out(a1, a4, a16, u, h, mod3, w_pool_b.reshape(w_pool[l].shape), pool_scale[l], w_out_b, p4t, p16t)
        h, _ = _ffn(h, mod3, g_ffn2[l], wg2, wu2, wd2, g_final, mod_row=6, final_norm=(l == depth - 1))
    return h
```

```python
import functools
import math

import jax
import jax.numpy as jnp
import numpy as np
from jax.experimental import pallas as pl
from jax.experimental.pallas import tpu as pltpu

D_MODEL = 1024
D_POOL = 512
D_ATTN = 512
POOL_WINDOWS = (2, 4, 8, 16)
POOL_GROUP_DIM = 128
N_POOL_GROUPS = len(POOL_WINDOWS)
HEAD_DIM = 64
N_HEADS = 8
WINDOW = 128
D_FF = 2816
N_MOD = 9
EPS = 1e-6
NEG = -1e30
LOG2E = math.log2(math.e)

LANES = 128
PERM_ROWS = 256
VMEM_LIMIT = 56 * 1024 * 1024

TM_FFN = 1024
TM_PROJ = 512
TM_MIX = 1024
ATTN_ROWS = 2048
BN_ADA = 1536

_f32 = jnp.float32
_bf16 = jnp.bfloat16


def _rms_mod(x, g, shift, scale):
    r = jax.lax.rsqrt(jnp.mean(x * x, axis=-1, keepdims=True) + EPS)
    return x * r * (g * (1.0 + scale)) + shift


def _resident(shape):
    nd = len(shape)
    return pl.BlockSpec(shape, lambda *_: (0,) * nd, pipeline_mode=pl.Buffered(1))


def _params(n_axes):
    return pltpu.CompilerParams(dimension_semantics=("arbitrary",) * n_axes, vmem_limit_bytes=VMEM_LIMIT)


def _ada_kernel(c_ref, w_ref, b_ref, o_ref):
    c = c_ref[...]
    a = (c * (1.0 / (1.0 + jnp.exp(-c)))).astype(_bf16)
    o_ref[...] = jnp.dot(a, w_ref[...].astype(_bf16), preferred_element_type=_f32) + b_ref[...]


def _ada_mod(c, w_ada, b_ada):
    B, D = c.shape
    N = w_ada.shape[1]
    return pl.pallas_call(
        _ada_kernel,
        grid=(N // BN_ADA,),
        in_specs=[pl.BlockSpec((B, D), lambda j: (0, 0)),
                  pl.BlockSpec((D, BN_ADA), lambda j: (0, j)),
                  pl.BlockSpec((1, BN_ADA), lambda j: (0, j))],
        out_specs=pl.BlockSpec((B, BN_ADA), lambda j: (0, j)),
        out_shape=jax.ShapeDtypeStruct((B, N), _f32),
        compiler_params=_params(1),
        name="ada_mod",
    )(c, w_ada, b_ada.reshape(1, N))


FF_CHUNK = 256
N_FFN_IN = 7
N_CAST_BLOCKS = 16


def _ffn_kernel(*refs, mod_row, final_norm, has_delta):
    x_ref, mod_ref, g_ref, wg_ref, wu_ref, wd_ref, gf_ref = refs[:N_FFN_IN]
    n_in = N_FFN_IN + int(has_delta)
    n_cast = (len(refs) - n_in - 1) // 2
    cast_in = refs[n_in:n_in + n_cast]
    o_ref = refs[n_in + n_cast]
    cast_out = refs[n_in + n_cast + 1:]
    shift = mod_ref[mod_row:mod_row + 1, :]
    scale = mod_ref[mod_row + 1:mod_row + 2, :]
    gate = mod_ref[mod_row + 2:mod_row + 3, :]
    x = x_ref[...] + refs[N_FFN_IN][...] if has_delta else x_ref[...]
    n = _rms_mod(x, g_ref[...], shift, scale).astype(_bf16)
    for src, dst in zip(cast_in, cast_out):
        dst[...] = src[...].astype(_bf16)
    acc = jnp.zeros(x.shape, _f32)
    for c in range(D_FF // FF_CHUNK):
        cs = slice(c * FF_CHUNK, (c + 1) * FF_CHUNK)
        g = jnp.dot(n, wg_ref[:, cs], preferred_element_type=_f32)
        u = jnp.dot(n, wu_ref[:, cs], preferred_element_type=_f32)
        a = (g * (1.0 / (1.0 + jnp.exp(-g))) * u).astype(_bf16)
        acc = acc + jnp.dot(a, wd_ref[cs, :], preferred_element_type=_f32)
    h = x + (0.5 * gate) * acc
    if final_norm:
        h = h * jax.lax.rsqrt(jnp.mean(h * h, axis=-1, keepdims=True) + EPS) * gf_ref[...]
    o_ref[...] = h


def _ffn(x, mod3, g, wg, wu, wd, g_final, *, mod_row, final_norm, delta=None, narrow=()):
    B, S, D = x.shape
    tm = TM_FFN
    steps = B * (S // tm)
    per_b = S // tm

    def cast_spec(w):
        every = steps // N_CAST_BLOCKS
        return pl.BlockSpec((w.shape[0] // N_CAST_BLOCKS, w.shape[1]),
                            lambda b, i: ((b * per_b + i) // every, 0))

    row_tile = pl.BlockSpec((None, tm, D), lambda b, i: (b, i, 0))
    extra = () if delta is None else (delta,)
    kern = functools.partial(_ffn_kernel, mod_row=mod_row, final_norm=final_norm, has_delta=delta is not None)
    outs = pl.pallas_call(
        kern,
        grid=(B, per_b),
        in_specs=[row_tile,
                  pl.BlockSpec((None, N_MOD, D), lambda b, i: (b, 0, 0)),
                  _resident((1, D)),
                  _resident(wg.shape),
                  _resident(wu.shape),
                  _resident(wd.shape),
                  _resident((1, D))] + [row_tile] * len(extra) + [cast_spec(w) for w in narrow],
        out_specs=[row_tile] + [cast_spec(w) for w in narrow],
        out_shape=[jax.ShapeDtypeStruct((B, S, D), _f32)]
                  + [jax.ShapeDtypeStruct(w.shape, _bf16) for w in narrow],
        compiler_params=_params(2),
        name="ffn_final" if final_norm else "ffn",
    )(x, mod3, g.reshape(1, D), wg, wu, wd, g_final.reshape(1, D), *extra, *narrow)
    return outs[0], outs[1:]


def _sort_matrix(dil):
    per = PERM_ROWS // dil
    i = np.arange(PERM_ROWS)
    src = (i % per) * dil + i // per
    p = np.zeros((PERM_ROWS, PERM_ROWS), np.float32)
    p[i, src] = 1.0
    return p


def _split3(x):
    hi = x.astype(_bf16)
    r1 = x - hi.astype(_f32)
    mid = r1.astype(_bf16)
    lo = (r1 - mid.astype(_f32)).astype(_bf16)
    return hi, mid, lo


def _inproj_kernel(h_ref, mod_ref, g_ref, w_ref, p4_ref, p16_ref,
                   u_ref, q1_ref, k1_ref, v1_ref, q4_ref, k4_ref, v4_ref, q16_ref, k16_ref, v16_ref):
    x = h_ref[...]
    tm = x.shape[0]
    n = _rms_mod(x, g_ref[...], mod_ref[3:4, :], mod_ref[4:5, :]).astype(_bf16)
    z = jnp.dot(n, w_ref[...], preferred_element_type=_f32)
    u_ref[...] = z[:, :D_POOL]
    q = (z[:, D_POOL:D_POOL + D_ATTN] * (HEAD_DIM ** -0.5 * LOG2E)).astype(_bf16)
    k = z[:, D_POOL + D_ATTN:D_POOL + 2 * D_ATTN].astype(_bf16)
    v = z[:, D_POOL + 2 * D_ATTN:].astype(_bf16)
    q1_ref[...] = q
    k1_ref[...] = k
    v1_ref[...] = v
    qkv = jnp.concatenate([q, k, v], axis=1)
    for dil, p_ref, outs in ((4, p4_ref, (q4_ref, k4_ref, v4_ref)),
                             (16, p16_ref, (q16_ref, k16_ref, v16_ref))):
        per = PERM_ROWS // dil
        for c in range(tm // PERM_ROWS):
            s = jnp.dot(p_ref[...], qkv[c * PERM_ROWS:(c + 1) * PERM_ROWS, :],
                        preferred_element_type=_f32).astype(_bf16)
            for a, o_ref in enumerate(outs):
                for cls in range(dil):
                    o_ref[cls, c * per:(c + 1) * per, :] = s[cls * per:(cls + 1) * per,
                                                             a * D_ATTN:(a + 1) * D_ATTN]


def _in_proj(h, mod3, g, w_in, p4, p16):
    B, S, D = h.shape
    tm = TM_PROJ
    nat = jax.ShapeDtypeStruct((B, S, D_ATTN), _bf16)
    nat_spec = pl.BlockSpec((None, tm, D_ATTN), lambda b, i: (b, i, 0))

    def cls_shape(dil):
        return jax.ShapeDtypeStruct((B, dil, S // dil, D_ATTN), _bf16)

    def cls_spec(dil):
        return pl.BlockSpec((None, dil, tm // dil, D_ATTN), lambda b, i: (b, 0, i, 0))

    return pl.pallas_call(
        _inproj_kernel,
        grid=(B, S // tm),
        in_specs=[pl.BlockSpec((None, tm, D), lambda b, i: (b, i, 0)),
                  pl.BlockSpec((None, N_MOD, D), lambda b, i: (b, 0, 0)),
                  _resident((1, D)),
                  _resident(w_in.shape),
                  _resident(p4.shape),
                  _resident(p16.shape)],
        out_specs=[pl.BlockSpec((None, tm, D_POOL), lambda b, i: (b, i, 0)),
                   nat_spec, nat_spec, nat_spec,
                   cls_spec(4), cls_spec(4), cls_spec(4),
                   cls_spec(16), cls_spec(16), cls_spec(16)],
        out_shape=[jax.ShapeDtypeStruct((B, S, D_POOL), _f32), nat, nat, nat,
                   cls_shape(4), cls_shape(4), cls_shape(4),
                   cls_shape(16), cls_shape(16), cls_shape(16)],
        compiler_params=_params(2),
        name="in_proj",
    )(h, mod3, g.reshape(1, D), w_in, p4, p16)


QB = 128


def _den_lane(h):
    return N_HEADS + h + (HEAD_DIM if h % 2 == 0 else 0)


def _attn_kernel(q_ref, km_ref, kh_ref, vm_ref, vh_ref, o_ref, st_ref, kbuf, vbuf, bias, *, dil):
    b, c, i = pl.program_id(0), pl.program_id(1), pl.program_id(2)
    n_cls, tq = q_ref.shape[0], q_ref.shape[1]

    @pl.when((b == 0) & (c == 0) & (i == 0))
    def _():
        row = jax.lax.broadcasted_iota(jnp.int32, (QB, 2 * QB), 0)
        col = jax.lax.broadcasted_iota(jnp.int32, (QB, 2 * QB), 1)
        delta = row - col + QB
        valid = (delta >= 0) & (delta <= WINDOW)
        dist = (delta * dil).astype(_f32)
        for h in range(N_HEADS):
            slope = 2.0 ** (-8.0 * (h + 1) / N_HEADS) * LOG2E
            bias[h] = jnp.where(valid, -slope * dist, NEG)

    kbuf[:, 0:QB, :] = kh_ref[...]
    kbuf[:, QB:, :] = km_ref[:, 0:QB, :]
    own_even = jax.lax.broadcasted_iota(jnp.int32, (1, 1, D_ATTN), 2) % LANES < HEAD_DIM
    for dst, src in ((slice(0, QB), vh_ref), (slice(QB, None), vm_ref)):
        v = src[...]
        ones = jnp.ones_like(v)
        vbuf[0, :, dst, :] = jnp.where(own_even, v, ones)
        vbuf[1, :, dst, :] = jnp.where(own_even, ones, v)

    lane = jax.lax.broadcasted_iota(jnp.int32, (QB, LANES), 1)
    low_half = lane < HEAD_DIM
    col = jax.lax.broadcasted_iota(jnp.int32, (1, 2 * QB), 1)
    no_prev = jnp.where((col < QB) & (i == 0), NEG, 0.0)

    for cls in range(n_cls):
        for j in range(tq // QB):
            rows = slice(j * QB, (j + 1) * QB)
            stats = jnp.zeros((QB, LANES), _f32)
            for hp in range(N_HEADS // 2):
                cs = slice(hp * LANES, (hp + 1) * LANES)
                q2 = q_ref[cls, rows, cs]
                k2 = kbuf[cls, :, cs] if j == 0 else km_ref[cls, (j - 1) * QB:(j + 1) * QB, cs]
                halves = []
                for e in range(2):
                    h = 2 * hp + e
                    qm = jnp.where(low_half if e == 0 else ~low_half, q2, jnp.zeros_like(q2))
                    s = jax.lax.dot_general(qm, k2, (((1,), (1,)), ((), ())), preferred_element_type=_f32)
                    s = s + bias[h]
                    if j == 0:
                        s = s + no_prev
                    m = jnp.max(s, axis=-1, keepdims=True)
                    p = jnp.exp2(s - m).astype(_bf16)
                    pv = jnp.dot(p, vbuf[e, cls, j * QB:(j + 2) * QB, cs], preferred_element_type=_f32)
                    halves.append(pv)
                    stats = jnp.where(lane == h, m, stats)
                    stats = jnp.where(lane == _den_lane(h), pv, stats)
                o_ref[cls, rows, cs] = jnp.where(low_half, halves[0], halves[1]).astype(_bf16)
            st_ref[cls, rows, :] = stats


def _attention(q, k, v, dil, tq_cap=ATTN_ROWS):
    B, C, L, _ = q.shape
    tq = min(L, tq_cap)
    n_cls = min(C, ATTN_ROWS // tq)

    def spec(rows, width, index):
        return pl.BlockSpec((None, n_cls, rows, width), index)

    def main_idx(b, c, i):
        return (b, c, i, 0)

    def halo_idx(b, c, i):
        return (b, c, jnp.maximum(i * (tq // QB) - 1, 0), 0)

    main, halo = spec(tq, D_ATTN, main_idx), spec(QB, D_ATTN, halo_idx)
    return pl.pallas_call(
        functools.partial(_attn_kernel, dil=dil),
        grid=(B, C // n_cls, L // tq),
        in_specs=[main, main, halo, main, halo],
        out_specs=[main, spec(tq, LANES, main_idx)],
        out_shape=[jax.ShapeDtypeStruct((B, C, L, D_ATTN), _bf16),
                   jax.ShapeDtypeStruct((B, C, L, LANES), _f32)],
        scratch_shapes=[pltpu.VMEM((n_cls, 2 * QB, D_ATTN), _bf16),
                        pltpu.VMEM((2, n_cls, tq + QB, D_ATTN), _bf16),
                        pltpu.VMEM((N_HEADS, QB, 2 * QB), _f32)],
        compiler_params=_params(3),
        name=f"attn_d{dil}",
    )(q, k, k, v, v)


POOL_HALO = 16


def _gather_classes(ref, c, dil):
    per = PERM_ROWS // dil
    return jnp.concatenate([ref[cls, c * per:(c + 1) * per, :] for cls in range(dil)], axis=0)


def _unsort_num(o_ref, pt_ref, c, dil):
    return jnp.dot(pt_ref[...], _gather_classes(o_ref, c, dil), preferred_element_type=_f32)


def _unsort_stats(st_ref, pt_ref, c, dil):
    parts = jnp.concatenate(_split3(_gather_classes(st_ref, c, dil)), axis=1)
    y = jnp.dot(pt_ref[...], parts, preferred_element_type=_f32)
    return y[:, :LANES] + y[:, LANES:2 * LANES] + y[:, 2 * LANES:]


def _pool_chunk(ubuf, wp_ref, ps_ref, c, t0):
    ext = PERM_ROWS + POOL_HALO
    t = t0 + jax.lax.broadcasted_iota(jnp.int32, (PERM_ROWS, 1), 0)
    ys = []
    for g, w in enumerate(POOL_WINDOWS):
        cs = slice(g * POOL_GROUP_DIM, (g + 1) * POOL_GROUP_DIM)
        x = ubuf[c * PERM_ROWS:c * PERM_ROWS + ext, cs]
        win = x
        back = 1
        while back < w:
            win = win + pltpu.roll(win, back, 0)
            back *= 2
        count = jnp.minimum(t + 1, w).astype(_f32)
        tok = x[POOL_HALO:, :]
        pooled = (win[POOL_HALO:, :] / count - tok).astype(_bf16)
        y = jnp.dot(pooled, wp_ref[g], preferred_element_type=_f32) * ps_ref[:, cs]
        ys.append(y.astype(_bf16))
    return jnp.concatenate(ys, axis=1)


def _merge_chunk(o1_ref, s1_ref, o4_ref, s4_ref, o16_ref, s16_ref, p4t_ref, p16t_ref, c):
    rows = slice(c * PERM_ROWS, (c + 1) * PERM_ROWS)
    nums = (o1_ref[rows, :].astype(_f32), _unsort_num(o4_ref, p4t_ref, c, 4),
            _unsort_num(o16_ref, p16t_ref, c, 16))
    stats = (s1_ref[rows, :], _unsort_stats(s4_ref, p4t_ref, c, 4), _unsort_stats(s16_ref, p16t_ref, c, 16))
    top = jnp.maximum(jnp.maximum(stats[0], stats[1]), stats[2])
    es = [jnp.exp2(s - top) for s in stats]
    even_head = jax.lax.broadcasted_iota(jnp.int32, (PERM_ROWS, LANES), 1) % 2 == 0
    dens = [jnp.where(even_head, pltpu.roll(s, LANES - _den_lane(0), 1), pltpu.roll(s, LANES - _den_lane(1) + 1, 1))
            for s in stats]
    inv = 1.0 / (es[0] * dens[0] + es[1] * dens[1] + es[2] * dens[2])
    wts = [e * inv for e in es]
    upper = (jax.lax.broadcasted_iota(jnp.int32, (PERM_ROWS, LANES), 1) >= HEAD_DIM).astype(jnp.int32)
    ys = []
    for hp in range(N_HEADS // 2):
        cs = slice(hp * LANES, (hp + 1) * LANES)
        src_lane = 2 * hp + upper
        y = None
        for r in range(3):
            w = jnp.take_along_axis(wts[r], src_lane, axis=1, mode="promise_in_bounds")
            y = w * nums[r][:, cs] if y is None else y + w * nums[r][:, cs]
        ys.append(y.astype(_bf16))
    return jnp.concatenate(ys, axis=1)


def _mixout_kernel(o1_ref, s1_ref, o4_ref, s4_ref, o16_ref, s16_ref, u_ref, uh_ref, mod_ref,
                   wp_ref, ps_ref, wo_ref, p4t_ref, p16t_ref, out_ref, ubuf):
    i = pl.program_id(1)
    tm = out_ref.shape[0]
    ubuf[0:POOL_HALO, :] = jnp.where(i == 0, 0.0, uh_ref[...])
    ubuf[POOL_HALO:, :] = u_ref[...]
    gate = mod_ref[5:6, :]

    def mixed(c):
        y_pool = _pool_chunk(ubuf, wp_ref, ps_ref, c, i * tm + c * PERM_ROWS)
        y_attn = _merge_chunk(o1_ref, s1_ref, o4_ref, s4_ref, o16_ref, s16_ref, p4t_ref, p16t_ref, c)
        return jnp.concatenate([y_pool, y_attn], axis=1)

    n_chunks = tm // PERM_ROWS
    ready = mixed(0)
    for c in range(n_chunks):
        rows = slice(c * PERM_ROWS, (c + 1) * PERM_ROWS)
        cur = ready
        if c + 1 < n_chunks:
            ready = mixed(c + 1)
        y = jnp.dot(cur, wo_ref[...], preferred_element_type=_f32)
        out_ref[rows, :] = gate * y


def _mix_out(a1, a4, a16, u, mod3, w_pool, pool_scale, w_out, p4t, p16t):
    B, S = u.shape[:2]
    D = w_out.shape[1]
    tm = TM_MIX

    def branch_specs(dil, width):
        if dil == 1:
            return pl.BlockSpec((None, None, tm, width), lambda b, i: (b, 0, i, 0))
        return pl.BlockSpec((None, dil, tm // dil, width), lambda b, i: (b, 0, i, 0))

    return pl.pallas_call(
        _mixout_kernel,
        grid=(B, S // tm),
        in_specs=[branch_specs(1, D_ATTN), branch_specs(1, LANES),
                  branch_specs(4, D_ATTN), branch_specs(4, LANES),
                  branch_specs(16, D_ATTN), branch_specs(16, LANES),
                  pl.BlockSpec((None, tm, D_POOL), lambda b, i: (b, i, 0)),
                  pl.BlockSpec((None, POOL_HALO, D_POOL),
                               lambda b, i: (b, jnp.maximum(i * (tm // POOL_HALO) - 1, 0), 0)),
                  pl.BlockSpec((None, N_MOD, D), lambda b, i: (b, 0, 0)),
                  _resident(w_pool.shape),
                  _resident((1, D_POOL)),
                  _resident(w_out.shape),
                  _resident(p4t.shape),
                  _resident(p16t.shape)],
        out_specs=pl.BlockSpec((None, tm, D), lambda b, i: (b, i, 0)),
        out_shape=jax.ShapeDtypeStruct((B, S, D), _f32),
        scratch_shapes=[pltpu.VMEM((tm + POOL_HALO, D_POOL), _f32)],
        compiler_params=_params(2),
        name="mix_out",
    )(*a1, *a4, *a16, u, u, mod3, w_pool, pool_scale.reshape(1, D_POOL), w_out, p4t, p16t)


def kernel(x, c, w_ada, b_ada, g_ffn1, w1_gate, w1_up, w1_down, g_mix, w_in, w_pool, pool_scale, w_out,
           g_ffn2, w2_gate, w2_up, w2_down, g_final):
    B, S, D = x.shape
    depth = w_ada.shape[0]
    p4 = jnp.asarray(_sort_matrix(4), _bf16)
    p16 = jnp.asarray(_sort_matrix(16), _bf16)
    p4t = jnp.asarray(_sort_matrix(4).T, _bf16)
    p16t = jnp.asarray(_sort_matrix(16).T, _bf16)
    h = x
    for l in range(depth):
        mod3 = _ada_mod(c, w_ada[l], b_ada[l]).reshape(B, N_MOD, D)
        pool_flat = w_pool[l].reshape(N_POOL_GROUPS * POOL_GROUP_DIM, POOL_GROUP_DIM)
        h, (wg2, wu2, wd2, w_in_b, w_out_b, w_pool_b) = _ffn(
            h, mod3, g_ffn1[l], w1_gate[l].astype(_bf16), w1_up[l].astype(_bf16), w1_down[l].astype(_bf16),
            g_final, mod_row=0, final_norm=False,
            narrow=(w2_gate[l], w2_up[l], w2_down[l], w_in[l], w_out[l], pool_flat))
        u, q1, k1, v1, q4, k4, v4, q16, k16, v16 = _in_proj(h, mod3, g_mix[l], w_in_b, p4, p16)
        a1 = _attention(q1[:, None], k1[:, None], v1[:, None], 1)
        a4 = _attention(q4, k4, v4, 4, tq_cap=ATTN_ROWS // 2)
        a16 = _attention(q16, k16, v16, 16)
        mixed = _mix_out(a1, a4, a16, u, mod3, w_pool_b.reshape(w_pool[l].shape), pool_scale[l], w_out_b, p4t, p16t)
        h, _ = _ffn(h, mod3, g_ffn2[l], wg2, wu2, wd2, g_final, mod_row=6, final_norm=(l == depth - 1), delta=mixed)
    return h
```

```python
import functools
import math

import jax
import jax.numpy as jnp
import numpy as np
from jax.experimental import pallas as pl
from jax.experimental.pallas import tpu as pltpu

D_MODEL = 1024
D_POOL = 512
D_ATTN = 512
POOL_WINDOWS = (2, 4, 8, 16)
POOL_GROUP_DIM = 128
N_POOL_GROUPS = len(POOL_WINDOWS)
HEAD_DIM = 64
N_HEADS = 8
WINDOW = 128
D_FF = 2816
N_MOD = 9
EPS = 1e-6
NEG = -1e30
LOG2E = math.log2(math.e)

LANES = 128
PERM_ROWS = 256
VMEM_LIMIT = 56 * 1024 * 1024

TM_FFN = 1024
TM_PROJ = 512
TM_MIX = 1024
ATTN_ROWS = 2048
BN_ADA = 1536

_f32 = jnp.float32
_bf16 = jnp.bfloat16


def _rms_mod(x, g, shift, scale):
    r = jax.lax.rsqrt(jnp.mean(x * x, axis=-1, keepdims=True) + EPS)
    return x * r * (g * (1.0 + scale)) + shift


def _resident(shape):
    nd = len(shape)
    return pl.BlockSpec(shape, lambda *_: (0,) * nd, pipeline_mode=pl.Buffered(1))


def _params(n_axes):
    return pltpu.CompilerParams(dimension_semantics=("arbitrary",) * n_axes, vmem_limit_bytes=VMEM_LIMIT)


def _ada_kernel(c_ref, w_ref, b_ref, o_ref):
    c = c_ref[...]
    a = (c * (1.0 / (1.0 + jnp.exp(-c)))).astype(_bf16)
    o_ref[...] = jnp.dot(a, w_ref[...].astype(_bf16), preferred_element_type=_f32) + b_ref[...]


def _ada_mod(c, w_ada, b_ada):
    B, D = c.shape
    N = w_ada.shape[1]
    return pl.pallas_call(
        _ada_kernel,
        grid=(N // BN_ADA,),
        in_specs=[pl.BlockSpec((B, D), lambda j: (0, 0)),
                  pl.BlockSpec((D, BN_ADA), lambda j: (0, j)),
                  pl.BlockSpec((1, BN_ADA), lambda j: (0, j))],
        out_specs=pl.BlockSpec((B, BN_ADA), lambda j: (0, j)),
        out_shape=jax.ShapeDtypeStruct((B, N), _f32),
        compiler_params=_params(1),
        name="ada_mod",
    )(c, w_ada, b_ada.reshape(1, N))


FF_CHUNK = 256
N_FFN_IN = 7
N_CAST_BLOCKS = 16


def _ffn_kernel(*refs, mod_row, final_norm, has_delta):
    x_ref, mod_ref, g_ref, wg_ref, wu_ref, wd_ref, gf_ref = refs[:N_FFN_IN]
    n_in = N_FFN_IN + int(has_delta)
    n_cast = (len(refs) - n_in - 1) // 2
    cast_in = refs[n_in:n_in + n_cast]
    o_ref = refs[n_in + n_cast]
    cast_out = refs[n_in + n_cast + 1:]
    shift = mod_ref[mod_row:mod_row + 1, :]
    scale = mod_ref[mod_row + 1:mod_row + 2, :]
    gate = mod_ref[mod_row + 2:mod_row + 3, :]
    x = x_ref[...] + refs[N_FFN_IN][...] if has_delta else x_ref[...]
    n = _rms_mod(x, g_ref[...], shift, scale).astype(_bf16)
    for src, dst in zip(cast_in, cast_out):
        dst[...] = src[...].astype(_bf16)
    acc = jnp.zeros(x.shape, _f32)
    for c in range(D_FF // FF_CHUNK):
        cs = slice(c * FF_CHUNK, (c + 1) * FF_CHUNK)
        g = jnp.dot(n, wg_ref[:, cs], preferred_element_type=_f32)
        u = jnp.dot(n, wu_ref[:, cs], preferred_element_type=_f32)
        a = (g * (1.0 / (1.0 + jnp.exp(-g))) * u).astype(_bf16)
        acc = acc + jnp.dot(a, wd_ref[cs, :], preferred_element_type=_f32)
    h = x + (0.5 * gate) * acc
    if final_norm:
        h = h * jax.lax.rsqrt(jnp.mean(h * h, axis=-1, keepdims=True) + EPS) * gf_ref[...]
    o_ref[...] = h


def _ffn(x, mod3, g, wg, wu, wd, g_final, *, mod_row, final_norm, delta=None, narrow=()):
    B, S, D = x.shape
    tm = TM_FFN
    steps = B * (S // tm)
    per_b = S // tm

    def cast_spec(w):
        every = steps // N_CAST_BLOCKS
        return pl.BlockSpec((w.shape[0] // N_CAST_BLOCKS, w.shape[1]),
                            lambda b, i: ((b * per_b + i) // every, 0))

    row_tile = pl.BlockSpec((None, tm, D), lambda b, i: (b, i, 0))
    extra = () if delta is None else (delta,)
    kern = functools.partial(_ffn_kernel, mod_row=mod_row, final_norm=final_norm, has_delta=delta is not None)
    outs = pl.pallas_call(
        kern,
        grid=(B, per_b),
        in_specs=[row_tile,
                  pl.BlockSpec((None, N_MOD, D), lambda b, i: (b, 0, 0)),
                  _resident((1, D)),
                  _resident(wg.shape),
                  _resident(wu.shape),
                  _resident(wd.shape),
                  _resident((1, D))] + [row_tile] * len(extra) + [cast_spec(w) for w in narrow],
        out_specs=[row_tile] + [cast_spec(w) for w in narrow],
        out_shape=[jax.ShapeDtypeStruct((B, S, D), _f32)]
                  + [jax.ShapeDtypeStruct(w.shape, _bf16) for w in narrow],
        compiler_params=_params(2),
        name="ffn_final" if final_norm else "ffn",
    )(x, mod3, g.reshape(1, D), wg, wu, wd, g_final.reshape(1, D), *extra, *narrow)
    return outs[0], outs[1:]


def _sort_matrix(dil):
    per = PERM_ROWS // dil
    i = np.arange(PERM_ROWS)
    src = (i % per) * dil + i // per
    p = np.zeros((PERM_ROWS, PERM_ROWS), np.float32)
    p[i, src] = 1.0
    return p


def _split3(x):
    hi = x.astype(_bf16)
    r1 = x - hi.astype(_f32)
    mid = r1.astype(_bf16)
    lo = (r1 - mid.astype(_f32)).astype(_bf16)
    return hi, mid, lo


def _inproj_kernel(h_ref, mod_ref, g_ref, w_ref, p4_ref, p16_ref,
                   u_ref, q1_ref, k1_ref, v1_ref, q4_ref, k4_ref, v4_ref, q16_ref, k16_ref, v16_ref):
    x = h_ref[...]
    tm = x.shape[0]
    n = _rms_mod(x, g_ref[...], mod_ref[3:4, :], mod_ref[4:5, :]).astype(_bf16)
    z = jnp.dot(n, w_ref[...], preferred_element_type=_f32)
    u_ref[...] = z[:, :D_POOL]
    q = (z[:, D_POOL:D_POOL + D_ATTN] * (HEAD_DIM ** -0.5 * LOG2E)).astype(_bf16)
    k = z[:, D_POOL + D_ATTN:D_POOL + 2 * D_ATTN].astype(_bf16)
    v = z[:, D_POOL + 2 * D_ATTN:].astype(_bf16)
    q1_ref[...] = q
    k1_ref[...] = k
    v1_ref[...] = v
    qkv = jnp.concatenate([q, k, v], axis=1)
    for dil, p_ref, outs in ((4, p4_ref, (q4_ref, k4_ref, v4_ref)),
                             (16, p16_ref, (q16_ref, k16_ref, v16_ref))):
        per = PERM_ROWS // dil
        for c in range(tm // PERM_ROWS):
            s = jnp.dot(p_ref[...], qkv[c * PERM_ROWS:(c + 1) * PERM_ROWS, :],
                        preferred_element_type=_f32).astype(_bf16)
            for a, o_ref in enumerate(outs):
                for cls in range(dil):
                    o_ref[cls, c * per:(c + 1) * per, :] = s[cls * per:(cls + 1) * per,
                                                             a * D_ATTN:(a + 1) * D_ATTN]


def _in_proj(h, mod3, g, w_in, p4, p16):
    B, S, D = h.shape
    tm = TM_PROJ
    nat = jax.ShapeDtypeStruct((B, S, D_ATTN), _bf16)
    nat_spec = pl.BlockSpec((None, tm, D_ATTN), lambda b, i: (b, i, 0))

    def cls_shape(dil):
        return jax.ShapeDtypeStruct((B, dil, S // dil, D_ATTN), _bf16)

    def cls_spec(dil):
        return pl.BlockSpec((None, dil, tm // dil, D_ATTN), lambda b, i: (b, 0, i, 0))

    return pl.pallas_call(
        _inproj_kernel,
        grid=(B, S // tm),
        in_specs=[pl.BlockSpec((None, tm, D), lambda b, i: (b, i, 0)),
                  pl.BlockSpec((None, N_MOD, D), lambda b, i: (b, 0, 0)),
                  _resident((1, D)),
                  _resident(w_in.shape),
                  _resident(p4.shape),
                  _resident(p16.shape)],
        out_specs=[pl.BlockSpec((None, tm, D_POOL), lambda b, i: (b, i, 0)),
                   nat_spec, nat_spec, nat_spec,
                   cls_spec(4), cls_spec(4), cls_spec(4),
                   cls_spec(16), cls_spec(16), cls_spec(16)],
        out_shape=[jax.ShapeDtypeStruct((B, S, D_POOL), _f32), nat, nat, nat,
                   cls_shape(4), cls_shape(4), cls_shape(4),
                   cls_shape(16), cls_shape(16), cls_shape(16)],
        compiler_params=_params(2),
        name="in_proj",
    )(h, mod3, g.reshape(1, D), w_in, p4, p16)


QB = 128


def _den_lane(h):
    return N_HEADS + h + (HEAD_DIM if h % 2 == 0 else 0)


def _attn_kernel(q_ref, km_ref, kh_ref, vm_ref, vh_ref, o_ref, st_ref, kbuf, vbuf, bias, *, dil, seq_len):
    b, i = pl.program_id(0), pl.program_id(1)
    tq = q_ref.shape[0]

    @pl.when((b == 0) & (i == 0))
    def _():
        row = jax.lax.broadcasted_iota(jnp.int32, (QB, 2 * QB), 0)
        col = jax.lax.broadcasted_iota(jnp.int32, (QB, 2 * QB), 1)
        delta = row - col + QB
        valid = (delta >= 0) & (delta <= WINDOW)
        dist = (delta * dil).astype(_f32)
        for h in range(N_HEADS):
            slope = 2.0 ** (-8.0 * (h + 1) / N_HEADS) * LOG2E
            bias[h] = jnp.where(valid, -slope * dist, NEG)

    kbuf[0:QB, :] = kh_ref[...]
    kbuf[QB:, :] = km_ref[0:QB, :]
    own_even = jax.lax.broadcasted_iota(jnp.int32, (1, D_ATTN), 1) % LANES < HEAD_DIM
    for dst, src in ((slice(0, QB), vh_ref), (slice(QB, None), vm_ref)):
        v = src[...]
        ones = jnp.ones_like(v)
        vbuf[0, dst, :] = jnp.where(own_even, v, ones)
        vbuf[1, dst, :] = jnp.where(own_even, ones, v)

    lane = jax.lax.broadcasted_iota(jnp.int32, (QB, LANES), 1)
    low_half = lane < HEAD_DIM
    prev_half = jax.lax.broadcasted_iota(jnp.int32, (1, 2 * QB), 1) < QB
    if seq_len >= tq:
        seq_start = {0: jnp.where(prev_half & (i % (seq_len // tq) == 0), NEG, 0.0)}
    else:
        seq_start = {j: jnp.where(prev_half, NEG, 0.0) for j in range(0, tq // QB, seq_len // QB)}

    for j in range(tq // QB):
        rows = slice(j * QB, (j + 1) * QB)
        stats = jnp.zeros((QB, LANES), _f32)
        for hp in range(N_HEADS // 2):
            cs = slice(hp * LANES, (hp + 1) * LANES)
            q2 = q_ref[rows, cs]
            k2 = kbuf[:, cs] if j == 0 else km_ref[(j - 1) * QB:(j + 1) * QB, cs]
            halves = []
            for e in range(2):
                h = 2 * hp + e
                qm = jnp.where(low_half if e == 0 else ~low_half, q2, jnp.zeros_like(q2))
                s = jax.lax.dot_general(qm, k2, (((1,), (1,)), ((), ())), preferred_element_type=_f32)
                s = s + bias[h]
                if j in seq_start:
                    s = s + seq_start[j]
                m = jnp.max(s, axis=-1, keepdims=True)
                p = jnp.exp2(s - m).astype(_bf16)
                pv = jnp.dot(p, vbuf[e, j * QB:(j + 2) * QB, cs], preferred_element_type=_f32)
                halves.append(pv)
                stats = jnp.where(lane == h, m, stats)
                stats = jnp.where(lane == _den_lane(h), pv, stats)
            o_ref[rows, cs] = jnp.where(low_half, halves[0], halves[1]).astype(_bf16)
        st_ref[rows, :] = stats


def _attention(q, k, v, dil):
    B, C, L, _ = q.shape
    S = C * L
    tq = ATTN_ROWS
    flat = [a.reshape(B, S, D_ATTN) for a in (q, k, v)]
    main = pl.BlockSpec((None, tq, D_ATTN), lambda b, i: (b, i, 0))
    halo = pl.BlockSpec((None, QB, D_ATTN), lambda b, i: (b, jnp.maximum(i * (tq // QB) - 1, 0), 0))
    num, stats = pl.pallas_call(
        functools.partial(_attn_kernel, dil=dil, seq_len=L),
        grid=(B, S // tq),
        in_specs=[main, main, halo, main, halo],
        out_specs=[main, pl.BlockSpec((None, tq, LANES), lambda b, i: (b, i, 0))],
        out_shape=[jax.ShapeDtypeStruct((B, S, D_ATTN), _bf16),
                   jax.ShapeDtypeStruct((B, S, LANES), _f32)],
        scratch_shapes=[pltpu.VMEM((2 * QB, D_ATTN), _bf16),
                        pltpu.VMEM((2, tq + QB, D_ATTN), _bf16),
                        pltpu.VMEM((N_HEADS, QB, 2 * QB), _f32)],
        compiler_params=_params(2),
        name=f"attn_d{dil}",
    )(flat[0], flat[1], flat[1], flat[2], flat[2])
    return num.reshape(B, C, L, D_ATTN), stats.reshape(B, C, L, LANES)


POOL_HALO = 16


def _gather_classes(ref, c, dil):
    per = PERM_ROWS // dil
    return jnp.concatenate([ref[cls, c * per:(c + 1) * per, :] for cls in range(dil)], axis=0)


def _unsort_num(o_ref, pt_ref, c, dil):
    return jnp.dot(pt_ref[...], _gather_classes(o_ref, c, dil), preferred_element_type=_f32)


def _unsort_stats(st_ref, pt_ref, c, dil):
    parts = jnp.concatenate(_split3(_gather_classes(st_ref, c, dil)), axis=1)
    y = jnp.dot(pt_ref[...], parts, preferred_element_type=_f32)
    return y[:, :LANES] + y[:, LANES:2 * LANES] + y[:, 2 * LANES:]


def _pool_chunk(ubuf, wp_ref, ps_ref, c, t0):
    ext = PERM_ROWS + POOL_HALO
    t = t0 + jax.lax.broadcasted_iota(jnp.int32, (PERM_ROWS, 1), 0)
    ys = []
    for g, w in enumerate(POOL_WINDOWS):
        cs = slice(g * POOL_GROUP_DIM, (g + 1) * POOL_GROUP_DIM)
        x = ubuf[c * PERM_ROWS:c * PERM_ROWS + ext, cs]
        win = x
        back = 1
        while back < w:
            win = win + pltpu.roll(win, back, 0)
            back *= 2
        count = jnp.minimum(t + 1, w).astype(_f32)
        tok = x[POOL_HALO:, :]
        pooled = (win[POOL_HALO:, :] / count - tok).astype(_bf16)
        y = jnp.dot(pooled, wp_ref[g], preferred_element_type=_f32) * ps_ref[:, cs]
        ys.append(y.astype(_bf16))
    return jnp.concatenate(ys, axis=1)


def _merge_chunk(o1_ref, s1_ref, o4_ref, s4_ref, o16_ref, s16_ref, p4t_ref, p16t_ref, c):
    rows = slice(c * PERM_ROWS, (c + 1) * PERM_ROWS)
    nums = (o1_ref[rows, :].astype(_f32), _unsort_num(o4_ref, p4t_ref, c, 4),
            _unsort_num(o16_ref, p16t_ref, c, 16))
    stats = (s1_ref[rows, :], _unsort_stats(s4_ref, p4t_ref, c, 4), _unsort_stats(s16_ref, p16t_ref, c, 16))
    top = jnp.maximum(jnp.maximum(stats[0], stats[1]), stats[2])
    es = [jnp.exp2(s - top) for s in stats]
    even_head = jax.lax.broadcasted_iota(jnp.int32, (PERM_ROWS, LANES), 1) % 2 == 0
    dens = [jnp.where(even_head, pltpu.roll(s, LANES - _den_lane(0), 1), pltpu.roll(s, LANES - _den_lane(1) + 1, 1))
            for s in stats]
    inv = 1.0 / (es[0] * dens[0] + es[1] * dens[1] + es[2] * dens[2])
    wts = [e * inv for e in es]
    upper = (jax.lax.broadcasted_iota(jnp.int32, (PERM_ROWS, LANES), 1) >= HEAD_DIM).astype(jnp.int32)
    ys = []
    for hp in range(N_HEADS // 2):
        cs = slice(hp * LANES, (hp + 1) * LANES)
        src_lane = 2 * hp + upper
        y = None
        for r in range(3):
            w = jnp.take_along_axis(wts[r], src_lane, axis=1, mode="promise_in_bounds")
            y = w * nums[r][:, cs] if y is None else y + w * nums[r][:, cs]
        ys.append(y.astype(_bf16))
    return jnp.concatenate(ys, axis=1)


def _mixout_kernel(o1_ref, s1_ref, o4_ref, s4_ref, o16_ref, s16_ref, u_ref, uh_ref, mod_ref,
                   wp_ref, ps_ref, wo_ref, p4t_ref, p16t_ref, out_ref, ubuf):
    i = pl.program_id(1)
    tm = out_ref.shape[0]
    ubuf[0:POOL_HALO, :] = jnp.where(i == 0, 0.0, uh_ref[...])
    ubuf[POOL_HALO:, :] = u_ref[...]
    gate = mod_ref[5:6, :]

    def mixed(c):
        y_pool = _pool_chunk(ubuf, wp_ref, ps_ref, c, i * tm + c * PERM_ROWS)
        y_attn = _merge_chunk(o1_ref, s1_ref, o4_ref, s4_ref, o16_ref, s16_ref, p4t_ref, p16t_ref, c)
        return jnp.concatenate([y_pool, y_attn], axis=1)

    n_chunks = tm // PERM_ROWS
    ready = mixed(0)
    for c in range(n_chunks):
        rows = slice(c * PERM_ROWS, (c + 1) * PERM_ROWS)
        cur = ready
        if c + 1 < n_chunks:
            ready = mixed(c + 1)
        y = jnp.dot(cur, wo_ref[...], preferred_element_type=_f32)
        out_ref[rows, :] = gate * y


def _mix_out(a1, a4, a16, u, mod3, w_pool, pool_scale, w_out, p4t, p16t):
    B, S = u.shape[:2]
    D = w_out.shape[1]
    tm = TM_MIX

    def branch_specs(dil, width):
        if dil == 1:
            return pl.BlockSpec((None, None, tm, width), lambda b, i: (b, 0, i, 0))
        return pl.BlockSpec((None, dil, tm // dil, width), lambda b, i: (b, 0, i, 0))

    return pl.pallas_call(
        _mixout_kernel,
        grid=(B, S // tm),
        in_specs=[branch_specs(1, D_ATTN), branch_specs(1, LANES),
                  branch_specs(4, D_ATTN), branch_specs(4, LANES),
                  branch_specs(16, D_ATTN), branch_specs(16, LANES),
                  pl.BlockSpec((None, tm, D_POOL), lambda b, i: (b, i, 0)),
                  pl.BlockSpec((None, POOL_HALO, D_POOL),
                               lambda b, i: (b, jnp.maximum(i * (tm // POOL_HALO) - 1, 0), 0)),
                  pl.BlockSpec((None, N_MOD, D), lambda b, i: (b, 0, 0)),
                  _resident(w_pool.shape),
                  _resident((1, D_POOL)),
                  _resident(w_out.shape),
                  _resident(p4t.shape),
                  _resident(p16t.shape)],
        out_specs=pl.BlockSpec((None, tm, D), lambda b, i: (b, i, 0)),
        out_shape=jax.ShapeDtypeStruct((B, S, D), _f32),
        scratch_shapes=[pltpu.VMEM((tm + POOL_HALO, D_POOL), _f32)],
        compiler_params=_params(2),
        name="mix_out",
    )(*a1, *a4, *a16, u, u, mod3, w_pool, pool_scale.reshape(1, D_POOL), w_out, p4t, p16t)


def kernel(x, c, w_ada, b_ada, g_ffn1, w1_gate, w1_up, w1_down, g_mix, w_in, w_pool, pool_scale, w_out,
           g_ffn2, w2_gate, w2_up, w2_down, g_final):
    B, S, D = x.shape
    depth = w_ada.shape[0]
    p4 = jnp.asarray(_sort_matrix(4), _bf16)
    p16 = jnp.asarray(_sort_matrix(16), _bf16)
    p4t = jnp.asarray(_sort_matrix(4).T, _bf16)
    p16t = jnp.asarray(_sort_matrix(16).T, _bf16)
    h = x
    for l in range(depth):
        mod3 = _ada_mod(c, w_ada[l], b_ada[l]).reshape(B, N_MOD, D)
        pool_flat = w_pool[l].reshape(N_POOL_GROUPS * POOL_GROUP_DIM, POOL_GROUP_DIM)
        h, (wg2, wu2, wd2, w_in_b, w_out_b, w_pool_b) = _ffn(
            h, mod3, g_ffn1[l], w1_gate[l].astype(_bf16), w1_up[l].astype(_bf16), w1_down[l].astype(_bf16),
            g_final, mod_row=0, final_norm=False,
            narrow=(w2_gate[l], w2_up[l], w2_down[l], w_in[l], w_out[l], pool_flat))
        u, q1, k1, v1, q4, k4, v4, q16, k16, v16 = _in_proj(h, mod3, g_mix[l], w_in_b, p4, p16)
        a1 = _attention(q1[:, None], k1[:, None], v1[:, None], 1)
        a4 = _attention(q4, k4, v4, 4)
        a16 = _attention(q16, k16, v16, 16)
        mixed = _mix_out(a1, a4, a16, u, mod3, w_pool_b.reshape(w_pool[l].shape), pool_scale[l], w_out_b, p4t, p16t)
        h, _ = _ffn(h, mod3, g_ffn2[l], wg2, wu2, wd2, g_final, mod_row=6, final_norm=(l == depth - 1), delta=mixed)
    return h
```

```python
import functools
import math

import jax
import jax.numpy as jnp
import numpy as np
from jax.experimental import pallas as pl
from jax.experimental.pallas import tpu as pltpu

D_MODEL = 1024
D_POOL = 512
D_ATTN = 512
POOL_WINDOWS = (2, 4, 8, 16)
POOL_GROUP_DIM = 128
N_POOL_GROUPS = len(POOL_WINDOWS)
HEAD_DIM = 64
N_HEADS = 8
WINDOW = 128
D_FF = 2816
N_MOD = 9
EPS = 1e-6
NEG = -1e30
LOG2E = math.log2(math.e)

LANES = 128
PERM_ROWS = 256
VMEM_LIMIT = 56 * 1024 * 1024

TM_FFN = 1024
TM_PROJ = 1024
TM_MIX = 1024
ATTN_ROWS = 2048
BN_ADA = 1536

_f32 = jnp.float32
_bf16 = jnp.bfloat16


def _rms_mod(x, g, shift, scale):
    r = jax.lax.rsqrt(jnp.mean(x * x, axis=-1, keepdims=True) + EPS)
    return x * r * (g * (1.0 + scale)) + shift


def _resident(shape):
    nd = len(shape)
    return pl.BlockSpec(shape, lambda *_: (0,) * nd, pipeline_mode=pl.Buffered(1))


def _params(n_axes):
    return pltpu.CompilerParams(dimension_semantics=("arbitrary",) * n_axes, vmem_limit_bytes=VMEM_LIMIT)


def _ada_kernel(c_ref, w_ref, b_ref, o_ref):
    c = c_ref[...]
    a = (c * (1.0 / (1.0 + jnp.exp(-c)))).astype(_bf16)
    o_ref[...] = jnp.dot(a, w_ref[...].astype(_bf16), preferred_element_type=_f32) + b_ref[...]


def _ada_mod(c, w_ada, b_ada):
    B, D = c.shape
    N = w_ada.shape[1]
    return pl.pallas_call(
        _ada_kernel,
        grid=(N // BN_ADA,),
        in_specs=[pl.BlockSpec((B, D), lambda j: (0, 0)),
                  pl.BlockSpec((D, BN_ADA), lambda j: (0, j)),
                  pl.BlockSpec((1, BN_ADA), lambda j: (0, j))],
        out_specs=pl.BlockSpec((B, BN_ADA), lambda j: (0, j)),
        out_shape=jax.ShapeDtypeStruct((B, N), _f32),
        compiler_params=_params(1),
        name="ada_mod",
    )(c, w_ada, b_ada.reshape(1, N))


FF_CHUNK = 256
N_FFN_IN = 7
N_CAST_BLOCKS = 16


def _ffn_kernel(*refs, mod_row, final_norm, has_delta):
    x_ref, mod_ref, g_ref, wg_ref, wu_ref, wd_ref, gf_ref = refs[:N_FFN_IN]
    n_in = N_FFN_IN + int(has_delta)
    n_cast = (len(refs) - n_in - 1) // 2
    cast_in = refs[n_in:n_in + n_cast]
    o_ref = refs[n_in + n_cast]
    cast_out = refs[n_in + n_cast + 1:]
    shift = mod_ref[mod_row:mod_row + 1, :]
    scale = mod_ref[mod_row + 1:mod_row + 2, :]
    gate = mod_ref[mod_row + 2:mod_row + 3, :]
    x = x_ref[...] + refs[N_FFN_IN][...] if has_delta else x_ref[...]
    n = _rms_mod(x, g_ref[...], shift, scale).astype(_bf16)
    for src, dst in zip(cast_in, cast_out):
        dst[...] = src[...].astype(_bf16)
    acc = jnp.zeros(x.shape, _f32)
    for c in range(D_FF // FF_CHUNK):
        cs = slice(c * FF_CHUNK, (c + 1) * FF_CHUNK)
        g = jnp.dot(n, wg_ref[:, cs], preferred_element_type=_f32)
        u = jnp.dot(n, wu_ref[:, cs], preferred_element_type=_f32)
        a = (g * (1.0 / (1.0 + jnp.exp(-g))) * u).astype(_bf16)
        acc = acc + jnp.dot(a, wd_ref[cs, :], preferred_element_type=_f32)
    h = x + (0.5 * gate) * acc
    if final_norm:
        h = h * jax.lax.rsqrt(jnp.mean(h * h, axis=-1, keepdims=True) + EPS) * gf_ref[...]
    o_ref[...] = h


def _ffn(x, mod3, g, wg, wu, wd, g_final, *, mod_row, final_norm, delta=None, narrow=()):
    B, S, D = x.shape
    tm = TM_FFN
    steps = B * (S // tm)
    per_b = S // tm

    def cast_spec(w):
        every = steps // N_CAST_BLOCKS
        return pl.BlockSpec((w.shape[0] // N_CAST_BLOCKS, w.shape[1]),
                            lambda b, i: ((b * per_b + i) // every, 0))

    row_tile = pl.BlockSpec((None, tm, D), lambda b, i: (b, i, 0))
    extra = () if delta is None else (delta,)
    kern = functools.partial(_ffn_kernel, mod_row=mod_row, final_norm=final_norm, has_delta=delta is not None)
    outs = pl.pallas_call(
        kern,
        grid=(B, per_b),
        in_specs=[row_tile,
                  pl.BlockSpec((None, N_MOD, D), lambda b, i: (b, 0, 0)),
                  _resident((1, D)),
                  _resident(wg.shape),
                  _resident(wu.shape),
                  _resident(wd.shape),
                  _resident((1, D))] + [row_tile] * len(extra) + [cast_spec(w) for w in narrow],
        out_specs=[row_tile] + [cast_spec(w) for w in narrow],
        out_shape=[jax.ShapeDtypeStruct((B, S, D), _f32)]
                  + [jax.ShapeDtypeStruct(w.shape, _bf16) for w in narrow],
        compiler_params=_params(2),
        name="ffn_final" if final_norm else "ffn",
    )(x, mod3, g.reshape(1, D), wg, wu, wd, g_final.reshape(1, D), *extra, *narrow)
    return outs[0], outs[1:]


def _sort_matrix(dil):
    per = PERM_ROWS // dil
    i = np.arange(PERM_ROWS)
    src = (i % per) * dil + i // per
    p = np.zeros((PERM_ROWS, PERM_ROWS), np.float32)
    p[i, src] = 1.0
    return p


def _split3(x):
    hi = x.astype(_bf16)
    r1 = x - hi.astype(_f32)
    mid = r1.astype(_bf16)
    lo = (r1 - mid.astype(_f32)).astype(_bf16)
    return hi, mid, lo


def _inproj_kernel(h_ref, mod_ref, g_ref, w_ref, p4_ref, p16_ref,
                   u_ref, q1_ref, k1_ref, v1_ref, q4_ref, k4_ref, v4_ref, q16_ref, k16_ref, v16_ref):
    x = h_ref[...]
    tm = x.shape[0]
    n = _rms_mod(x, g_ref[...], mod_ref[3:4, :], mod_ref[4:5, :]).astype(_bf16)
    z = jnp.dot(n, w_ref[...], preferred_element_type=_f32)
    u_ref[...] = z[:, :D_POOL]
    q = (z[:, D_POOL:D_POOL + D_ATTN] * (HEAD_DIM ** -0.5 * LOG2E)).astype(_bf16)
    k = z[:, D_POOL + D_ATTN:D_POOL + 2 * D_ATTN].astype(_bf16)
    v = z[:, D_POOL + 2 * D_ATTN:].astype(_bf16)
    q1_ref[...] = q
    k1_ref[...] = k
    v1_ref[...] = v
    qkv = jnp.concatenate([q, k, v], axis=1)
    for dil, p_ref, outs in ((4, p4_ref, (q4_ref, k4_ref, v4_ref)),
                             (16, p16_ref, (q16_ref, k16_ref, v16_ref))):
        per = PERM_ROWS // dil
        for c in range(tm // PERM_ROWS):
            s = jnp.dot(p_ref[...], qkv[c * PERM_ROWS:(c + 1) * PERM_ROWS, :],
                        preferred_element_type=_f32).astype(_bf16)
            for a, o_ref in enumerate(outs):
                for cls in range(dil):
                    o_ref[cls, c * per:(c + 1) * per, :] = s[cls * per:(cls + 1) * per,
                                                             a * D_ATTN:(a + 1) * D_ATTN]


def _in_proj(h, mod3, g, w_in, p4, p16):
    B, S, D = h.shape
    tm = TM_PROJ
    nat = jax.ShapeDtypeStruct((B, S, D_ATTN), _bf16)
    nat_spec = pl.BlockSpec((None, tm, D_ATTN), lambda b, i: (b, i, 0))

    def cls_shape(dil):
        return jax.ShapeDtypeStruct((B, dil, S // dil, D_ATTN), _bf16)

    def cls_spec(dil):
        return pl.BlockSpec((None, dil, tm // dil, D_ATTN), lambda b, i: (b, 0, i, 0))

    return pl.pallas_call(
        _inproj_kernel,
        grid=(B, S // tm),
        in_specs=[pl.BlockSpec((None, tm, D), lambda b, i: (b, i, 0)),
                  pl.BlockSpec((None, N_MOD, D), lambda b, i: (b, 0, 0)),
                  _resident((1, D)),
                  _resident(w_in.shape),
                  _resident(p4.shape),
                  _resident(p16.shape)],
        out_specs=[pl.BlockSpec((None, tm, D_POOL), lambda b, i: (b, i, 0)),
                   nat_spec, nat_spec, nat_spec,
                   cls_spec(4), cls_spec(4), cls_spec(4),
                   cls_spec(16), cls_spec(16), cls_spec(16)],
        out_shape=[jax.ShapeDtypeStruct((B, S, D_POOL), _f32), nat, nat, nat,
                   cls_shape(4), cls_shape(4), cls_shape(4),
                   cls_shape(16), cls_shape(16), cls_shape(16)],
        compiler_params=_params(2),
        name="in_proj",
    )(h, mod3, g.reshape(1, D), w_in, p4, p16)


QB = 128


def _den_lane(h):
    return N_HEADS + h + (HEAD_DIM if h % 2 == 0 else 0)


def _attn_kernel(q_ref, km_ref, kh_ref, vm_ref, vh_ref, o_ref, st_ref, kbuf, vbuf, bias, *, dil, seq_len):
    b, i = pl.program_id(0), pl.program_id(1)
    tq = q_ref.shape[0]

    @pl.when((b == 0) & (i == 0))
    def _():
        row = jax.lax.broadcasted_iota(jnp.int32, (QB, 2 * QB), 0)
        col = jax.lax.broadcasted_iota(jnp.int32, (QB, 2 * QB), 1)
        delta = row - col + QB
        valid = (delta >= 0) & (delta <= WINDOW)
        dist = (delta * dil).astype(_f32)
        for h in range(N_HEADS):
            slope = 2.0 ** (-8.0 * (h + 1) / N_HEADS) * LOG2E
            bias[h] = jnp.where(valid, -slope * dist, NEG)

    kbuf[0:QB, :] = kh_ref[...]
    kbuf[QB:, :] = km_ref[0:QB, :]
    own_even = jax.lax.broadcasted_iota(jnp.int32, (1, D_ATTN), 1) % LANES < HEAD_DIM
    for dst, src in ((slice(0, QB), vh_ref), (slice(QB, None), vm_ref)):
        v = src[...]
        ones = jnp.ones_like(v)
        vbuf[0, dst, :] = jnp.where(own_even, v, ones)
        vbuf[1, dst, :] = jnp.where(own_even, ones, v)

    lane = jax.lax.broadcasted_iota(jnp.int32, (QB, LANES), 1)
    low_half = lane < HEAD_DIM
    prev_half = jax.lax.broadcasted_iota(jnp.int32, (1, 2 * QB), 1) < QB
    if seq_len >= tq:
        seq_start = {0: jnp.where(prev_half & (i % (seq_len // tq) == 0), NEG, 0.0)}
    else:
        seq_start = {j: jnp.where(prev_half, NEG, 0.0) for j in range(0, tq // QB, seq_len // QB)}

    for j in range(tq // QB):
        rows = slice(j * QB, (j + 1) * QB)
        stats = jnp.zeros((QB, LANES), _f32)
        for hp in range(N_HEADS // 2):
            cs = slice(hp * LANES, (hp + 1) * LANES)
            q2 = q_ref[rows, cs]
            k2 = kbuf[:, cs] if j == 0 else km_ref[(j - 1) * QB:(j + 1) * QB, cs]
            halves = []
            for e in range(2):
                h = 2 * hp + e
                qm = jnp.where(low_half if e == 0 else ~low_half, q2, jnp.zeros_like(q2))
                s = jax.lax.dot_general(qm, k2, (((1,), (1,)), ((), ())), preferred_element_type=_f32)
                s = s + bias[h]
                if j in seq_start:
                    s = s + seq_start[j]
                m = jnp.max(s, axis=-1, keepdims=True)
                p = jnp.exp2(s - m).astype(_bf16)
                pv = jnp.dot(p, vbuf[e, j * QB:(j + 2) * QB, cs], preferred_element_type=_f32)
                halves.append(pv)
                stats = jnp.where(lane == h, m, stats)
                stats = jnp.where(lane == _den_lane(h), pv, stats)
            o_ref[rows, cs] = jnp.where(low_half, halves[0], halves[1]).astype(_bf16)
        st_ref[rows, :] = stats


def _attention(q, k, v, dil):
    B, C, L, _ = q.shape
    S = C * L
    tq = ATTN_ROWS
    flat = [a.reshape(B, S, D_ATTN) for a in (q, k, v)]
    main = pl.BlockSpec((None, tq, D_ATTN), lambda b, i: (b, i, 0))
    halo = pl.BlockSpec((None, QB, D_ATTN), lambda b, i: (b, jnp.maximum(i * (tq // QB) - 1, 0), 0))
    num, stats = pl.pallas_call(
        functools.partial(_attn_kernel, dil=dil, seq_len=L),
        grid=(B, S // tq),
        in_specs=[main, main, halo, main, halo],
        out_specs=[main, pl.BlockSpec((None, tq, LANES), lambda b, i: (b, i, 0))],
        out_shape=[jax.ShapeDtypeStruct((B, S, D_ATTN), _bf16),
                   jax.ShapeDtypeStruct((B, S, LANES), _f32)],
        scratch_shapes=[pltpu.VMEM((2 * QB, D_ATTN), _bf16),
                        pltpu.VMEM((2, tq + QB, D_ATTN), _bf16),
                        pltpu.VMEM((N_HEADS, QB, 2 * QB), _f32)],
        compiler_params=_params(2),
        name=f"attn_d{dil}",
    )(flat[0], flat[1], flat[1], flat[2], flat[2])
    return num.reshape(B, C, L, D_ATTN), stats.reshape(B, C, L, LANES)


POOL_HALO = 16


def _gather_classes(ref, c, dil):
    per = PERM_ROWS // dil
    return jnp.concatenate([ref[cls, c * per:(c + 1) * per, :] for cls in range(dil)], axis=0)


def _unsort_num(o_ref, pt_ref, c, dil):
    return jnp.dot(pt_ref[...], _gather_classes(o_ref, c, dil), preferred_element_type=_f32)


def _unsort_stats(st_ref, pt_ref, c, dil):
    parts = jnp.concatenate(_split3(_gather_classes(st_ref, c, dil)), axis=1)
    y = jnp.dot(pt_ref[...], parts, preferred_element_type=_f32)
    return y[:, :LANES] + y[:, LANES:2 * LANES] + y[:, 2 * LANES:]


def _pool_chunk(ubuf, wp_ref, ps_ref, c, t0):
    ext = PERM_ROWS + POOL_HALO
    t = t0 + jax.lax.broadcasted_iota(jnp.int32, (PERM_ROWS, 1), 0)
    ys = []
    for g, w in enumerate(POOL_WINDOWS):
        cs = slice(g * POOL_GROUP_DIM, (g + 1) * POOL_GROUP_DIM)
        x = ubuf[c * PERM_ROWS:c * PERM_ROWS + ext, cs]
        win = x
        back = 1
        while back < w:
            win = win + pltpu.roll(win, back, 0)
            back *= 2
        count = jnp.minimum(t + 1, w).astype(_f32)
        tok = x[POOL_HALO:, :]
        pooled = (win[POOL_HALO:, :] / count - tok).astype(_bf16)
        y = jnp.dot(pooled, wp_ref[g], preferred_element_type=_f32) * ps_ref[:, cs]
        ys.append(y.astype(_bf16))
    return jnp.concatenate(ys, axis=1)


def _merge_chunk(o1_ref, s1_ref, o4_ref, s4_ref, o16_ref, s16_ref, p4t_ref, p16t_ref, c):
    rows = slice(c * PERM_ROWS, (c + 1) * PERM_ROWS)
    nums = (o1_ref[rows, :].astype(_f32), _unsort_num(o4_ref, p4t_ref, c, 4),
            _unsort_num(o16_ref, p16t_ref, c, 16))
    stats = (s1_ref[rows, :], _unsort_stats(s4_ref, p4t_ref, c, 4), _unsort_stats(s16_ref, p16t_ref, c, 16))
    top = jnp.maximum(jnp.maximum(stats[0], stats[1]), stats[2])
    es = [jnp.exp2(s - top) for s in stats]
    even_head = jax.lax.broadcasted_iota(jnp.int32, (PERM_ROWS, LANES), 1) % 2 == 0
    dens = [jnp.where(even_head, pltpu.roll(s, LANES - _den_lane(0), 1), pltpu.roll(s, LANES - _den_lane(1) + 1, 1))
            for s in stats]
    inv = 1.0 / (es[0] * dens[0] + es[1] * dens[1] + es[2] * dens[2])
    wts = [e * inv for e in es]
    upper = (jax.lax.broadcasted_iota(jnp.int32, (PERM_ROWS, LANES), 1) >= HEAD_DIM).astype(jnp.int32)
    ys = []
    for hp in range(N_HEADS // 2):
        cs = slice(hp * LANES, (hp + 1) * LANES)
        src_lane = 2 * hp + upper
        y = None
        for r in range(3):
            w = jnp.take_along_axis(wts[r], src_lane, axis=1, mode="promise_in_bounds")
            y = w * nums[r][:, cs] if y is None else y + w * nums[r][:, cs]
        ys.append(y.astype(_bf16))
    return jnp.concatenate(ys, axis=1)


def _mixout_kernel(o1_ref, s1_ref, o4_ref, s4_ref, o16_ref, s16_ref, u_ref, uh_ref, mod_ref,
                   wp_ref, ps_ref, wo_ref, p4t_ref, p16t_ref, out_ref, ubuf):
    i = pl.program_id(1)
    tm = out_ref.shape[0]
    ubuf[0:POOL_HALO, :] = jnp.where(i == 0, 0.0, uh_ref[...])
    ubuf[POOL_HALO:, :] = u_ref[...]
    gate = mod_ref[5:6, :]

    def mixed(c):
        y_pool = _pool_chunk(ubuf, wp_ref, ps_ref, c, i * tm + c * PERM_ROWS)
        y_attn = _merge_chunk(o1_ref, s1_ref, o4_ref, s4_ref, o16_ref, s16_ref, p4t_ref, p16t_ref, c)
        return jnp.concatenate([y_pool, y_attn], axis=1)

    n_chunks = tm // PERM_ROWS
    ready = mixed(0)
    for c in range(n_chunks):
        rows = slice(c * PERM_ROWS, (c + 1) * PERM_ROWS)
        cur = ready
        if c + 1 < n_chunks:
            ready = mixed(c + 1)
        y = jnp.dot(cur, wo_ref[...], preferred_element_type=_f32)
        out_ref[rows, :] = gate * y


def _mix_out(a1, a4, a16, u, mod3, w_pool, pool_scale, w_out, p4t, p16t):
    B, S = u.shape[:2]
    D = w_out.shape[1]
    tm = TM_MIX

    def branch_specs(dil, width):
        if dil == 1:
            return pl.BlockSpec((None, None, tm, width), lambda b, i: (b, 0, i, 0))
        return pl.BlockSpec((None, dil, tm // dil, width), lambda b, i: (b, 0, i, 0))

    return pl.pallas_call(
        _mixout_kernel,
        grid=(B, S // tm),
        in_specs=[branch_specs(1, D_ATTN), branch_specs(1, LANES),
                  branch_specs(4, D_ATTN), branch_specs(4, LANES),
                  branch_specs(16, D_ATTN), branch_specs(16, LANES),
                  pl.BlockSpec((None, tm, D_POOL), lambda b, i: (b, i, 0)),
                  pl.BlockSpec((None, POOL_HALO, D_POOL),
                               lambda b, i: (b, jnp.maximum(i * (tm // POOL_HALO) - 1, 0), 0)),
                  pl.BlockSpec((None, N_MOD, D), lambda b, i: (b, 0, 0)),
                  _resident(w_pool.shape),
                  _resident((1, D_POOL)),
                  _resident(w_out.shape),
                  _resident(p4t.shape),
                  _resident(p16t.shape)],
        out_specs=pl.BlockSpec((None, tm, D), lambda b, i: (b, i, 0)),
        out_shape=jax.ShapeDtypeStruct((B, S, D), _f32),
        scratch_shapes=[pltpu.VMEM((tm + POOL_HALO, D_POOL), _f32)],
        compiler_params=_params(2),
        name="mix_out",
    )(*a1, *a4, *a16, u, u, mod3, w_pool, pool_scale.reshape(1, D_POOL), w_out, p4t, p16t)


def kernel(x, c, w_ada, b_ada, g_ffn1, w1_gate, w1_up, w1_down, g_mix, w_in, w_pool, pool_scale, w_out,
           g_ffn2, w2_gate, w2_up, w2_down, g_final):
    B, S, D = x.shape
    depth = w_ada.shape[0]
    p4 = jnp.asarray(_sort_matrix(4), _bf16)
    p16 = jnp.asarray(_sort_matrix(16), _bf16)
    p4t = jnp.asarray(_sort_matrix(4).T, _bf16)
    p16t = jnp.asarray(_sort_matrix(16).T, _bf16)
    h = x
    for l in range(depth):
        mod3 = _ada_mod(c, w_ada[l], b_ada[l]).reshape(B, N_MOD, D)
        pool_flat = w_pool[l].reshape(N_POOL_GROUPS * POOL_GROUP_DIM, POOL_GROUP_DIM)
        h, (wg2, wu2, wd2, w_in_b, w_out_b, w_pool_b) = _ffn(
            h, mod3, g_ffn1[l], w1_gate[l].astype(_bf16), w1_up[l].astype(_bf16), w1_down[l].astype(_bf16),
            g_final, mod_row=0, final_norm=False,
            narrow=(w2_gate[l], w2_up[l], w2_down[l], w_in[l], w_out[l], pool_flat))
        u, q1, k1, v1, q4, k4, v4, q16, k16, v16 = _in_proj(h, mod3, g_mix[l], w_in_b, p4, p16)
        a4 = _attention(q4, k4, v4, 4)
        a16 = _attention(q16, k16, v16, 16)
        a1 = _attention(q1[:, None], k1[:, None], v1[:, None], 1)
        mixed = _mix_out(a1, a4, a16, u, mod3, w_pool_b.reshape(w_pool[l].shape), pool_scale[l], w_out_b, p4t, p16t)
        h, _ = _ffn(h, mod3, g_ffn2[l], wg2, wu2, wd2, g_final, mod_row=6, final_norm=(l == depth - 1), delta=mixed)
    return h
```

```python
import functools
import math

import jax
import jax.numpy as jnp
import numpy as np
from jax.experimental import pallas as pl
from jax.experimental.pallas import tpu as pltpu

D_MODEL = 1024
D_POOL = 512
D_ATTN = 512
POOL_WINDOWS = (2, 4, 8, 16)
POOL_GROUP_DIM = 128
N_POOL_GROUPS = len(POOL_WINDOWS)
HEAD_DIM = 64
N_HEADS = 8
WINDOW = 128
D_FF = 2816
N_MOD = 9
EPS = 1e-6
NEG = -1e30
LOG2E = math.log2(math.e)

LANES = 128
PERM_ROWS = 256
VMEM_LIMIT = 56 * 1024 * 1024

TM_FFN = 1024
TM_PROJ = 1024
TM_MIX = 1024
ATTN_ROWS = 2048
BN_ADA = 1152

_f32 = jnp.float32
_bf16 = jnp.bfloat16


def _rms_mod(x, g, shift, scale):
    r = jax.lax.rsqrt(jnp.mean(x * x, axis=-1, keepdims=True) + EPS)
    return x * r * (g * (1.0 + scale)) + shift


def _resident(shape):
    nd = len(shape)
    return pl.BlockSpec(shape, lambda *_: (0,) * nd, pipeline_mode=pl.Buffered(1))


def _params(n_axes):
    return pltpu.CompilerParams(dimension_semantics=("arbitrary",) * n_axes, vmem_limit_bytes=VMEM_LIMIT)


def _ada_kernel(c_ref, w_ref, b_ref, *refs):
    n_cast = len(refs) // 2
    o_ref = refs[n_cast]
    c = c_ref[...]
    a = (c * (1.0 / (1.0 + jnp.exp(-c)))).astype(_bf16)
    o_ref[...] = jnp.dot(a, w_ref[...].astype(_bf16), preferred_element_type=_f32) + b_ref[...]
    for src, dst in zip(refs[:n_cast], refs[n_cast + 1:]):
        dst[...] = src[...].astype(_bf16)


def _ada_mod(c, w_ada, b_ada, narrow):
    B, D = c.shape
    N = w_ada.shape[1]
    steps = N // BN_ADA

    def cast_spec(w):
        return pl.BlockSpec((w.shape[0] // steps, w.shape[1]), lambda j: (j, 0))

    outs = pl.pallas_call(
        _ada_kernel,
        grid=(steps,),
        in_specs=[pl.BlockSpec((B, D), lambda j: (0, 0)),
                  pl.BlockSpec((D, BN_ADA), lambda j: (0, j)),
                  pl.BlockSpec((1, BN_ADA), lambda j: (0, j))] + [cast_spec(w) for w in narrow],
        out_specs=[pl.BlockSpec((B, BN_ADA), lambda j: (0, j))] + [cast_spec(w) for w in narrow],
        out_shape=[jax.ShapeDtypeStruct((B, N), _f32)] + [jax.ShapeDtypeStruct(w.shape, _bf16) for w in narrow],
        compiler_params=_params(1),
        name="ada_mod",
    )(c, w_ada, b_ada.reshape(1, N), *narrow)
    return outs[0], outs[1:]


FF_CHUNK = 256
N_FFN_IN = 7
N_CAST_BLOCKS = 16


def _ffn_kernel(*refs, mod_row, final_norm, has_delta):
    x_ref, mod_ref, g_ref, wg_ref, wu_ref, wd_ref, gf_ref = refs[:N_FFN_IN]
    n_in = N_FFN_IN + int(has_delta)
    n_cast = (len(refs) - n_in - 1) // 2
    cast_in = refs[n_in:n_in + n_cast]
    o_ref = refs[n_in + n_cast]
    cast_out = refs[n_in + n_cast + 1:]
    shift = mod_ref[mod_row:mod_row + 1, :]
    scale = mod_ref[mod_row + 1:mod_row + 2, :]
    gate = mod_ref[mod_row + 2:mod_row + 3, :]
    x = x_ref[...] + refs[N_FFN_IN][...] if has_delta else x_ref[...]
    n = _rms_mod(x, g_ref[...], shift, scale).astype(_bf16)
    for src, dst in zip(cast_in, cast_out):
        dst[...] = src[...].astype(_bf16)
    acc = jnp.zeros(x.shape, _f32)
    for c in range(D_FF // FF_CHUNK):
        cs = slice(c * FF_CHUNK, (c + 1) * FF_CHUNK)
        g = jnp.dot(n, wg_ref[:, cs], preferred_element_type=_f32)
        u = jnp.dot(n, wu_ref[:, cs], preferred_element_type=_f32)
        a = (g * (1.0 / (1.0 + jnp.exp(-g))) * u).astype(_bf16)
        acc = acc + jnp.dot(a, wd_ref[cs, :], preferred_element_type=_f32)
    h = x + (0.5 * gate) * acc
    if final_norm:
        h = h * jax.lax.rsqrt(jnp.mean(h * h, axis=-1, keepdims=True) + EPS) * gf_ref[...]
    o_ref[...] = h


def _ffn(x, mod3, g, wg, wu, wd, g_final, *, mod_row, final_norm, delta=None, narrow=()):
    B, S, D = x.shape
    tm = TM_FFN
    steps = B * (S // tm)
    per_b = S // tm

    def cast_spec(w):
        every = steps // N_CAST_BLOCKS
        return pl.BlockSpec((w.shape[0] // N_CAST_BLOCKS, w.shape[1]),
                            lambda b, i: ((b * per_b + i) // every, 0))

    row_tile = pl.BlockSpec((None, tm, D), lambda b, i: (b, i, 0))
    extra = () if delta is None else (delta,)
    kern = functools.partial(_ffn_kernel, mod_row=mod_row, final_norm=final_norm, has_delta=delta is not None)
    outs = pl.pallas_call(
        kern,
        grid=(B, per_b),
        in_specs=[row_tile,
                  pl.BlockSpec((None, N_MOD, D), lambda b, i: (b, 0, 0)),
                  _resident((1, D)),
                  _resident(wg.shape),
                  _resident(wu.shape),
                  _resident(wd.shape),
                  _resident((1, D))] + [row_tile] * len(extra) + [cast_spec(w) for w in narrow],
        out_specs=[row_tile] + [cast_spec(w) for w in narrow],
        out_shape=[jax.ShapeDtypeStruct((B, S, D), _f32)]
                  + [jax.ShapeDtypeStruct(w.shape, _bf16) for w in narrow],
        compiler_params=_params(2),
        name="ffn_final" if final_norm else "ffn",
    )(x, mod3, g.reshape(1, D), wg, wu, wd, g_final.reshape(1, D), *extra, *narrow)
    return outs[0], outs[1:]


def _sort_matrix(dil):
    per = PERM_ROWS // dil
    i = np.arange(PERM_ROWS)
    src = (i % per) * dil + i // per
    p = np.zeros((PERM_ROWS, PERM_ROWS), np.float32)
    p[i, src] = 1.0
    return p


def _split3(x):
    hi = x.astype(_bf16)
    r1 = x - hi.astype(_f32)
    mid = r1.astype(_bf16)
    lo = (r1 - mid.astype(_f32)).astype(_bf16)
    return hi, mid, lo


def _inproj_kernel(h_ref, mod_ref, g_ref, w_ref, p4_ref, p16_ref,
                   u_ref, q1_ref, k1_ref, v1_ref, q16_ref, k16_ref, v16_ref, q4_ref, k4_ref, v4_ref):
    x = h_ref[...]
    tm = x.shape[0]
    n = _rms_mod(x, g_ref[...], mod_ref[3:4, :], mod_ref[4:5, :]).astype(_bf16)
    z = jnp.dot(n, w_ref[...], preferred_element_type=_f32)
    u_ref[...] = z[:, :D_POOL]
    q = (z[:, D_POOL:D_POOL + D_ATTN] * (HEAD_DIM ** -0.5 * LOG2E)).astype(_bf16)
    k = z[:, D_POOL + D_ATTN:D_POOL + 2 * D_ATTN].astype(_bf16)
    v = z[:, D_POOL + 2 * D_ATTN:].astype(_bf16)
    q1_ref[...] = q
    k1_ref[...] = k
    v1_ref[...] = v
    qkv = jnp.concatenate([q, k, v], axis=1)
    for dil, p_ref, outs in ((4, p4_ref, (q4_ref, k4_ref, v4_ref)),
                             (16, p16_ref, (q16_ref, k16_ref, v16_ref))):
        per = PERM_ROWS // dil
        for c in range(tm // PERM_ROWS):
            s = jnp.dot(p_ref[...], qkv[c * PERM_ROWS:(c + 1) * PERM_ROWS, :],
                        preferred_element_type=_f32).astype(_bf16)
            for a, o_ref in enumerate(outs):
                for cls in range(dil):
                    o_ref[cls, c * per:(c + 1) * per, :] = s[cls * per:(cls + 1) * per,
                                                             a * D_ATTN:(a + 1) * D_ATTN]


def _in_proj(h, mod3, g, w_in, p4, p16):
    B, S, D = h.shape
    tm = TM_PROJ
    nat = jax.ShapeDtypeStruct((B, S, D_ATTN), _bf16)
    nat_spec = pl.BlockSpec((None, tm, D_ATTN), lambda b, i: (b, i, 0))

    def cls_shape(dil):
        return jax.ShapeDtypeStruct((B, dil, S // dil, D_ATTN), _bf16)

    def cls_spec(dil):
        return pl.BlockSpec((None, dil, tm // dil, D_ATTN), lambda b, i: (b, 0, i, 0))

    return pl.pallas_call(
        _inproj_kernel,
        grid=(B, S // tm),
        in_specs=[pl.BlockSpec((None, tm, D), lambda b, i: (b, i, 0)),
                  pl.BlockSpec((None, N_MOD, D), lambda b, i: (b, 0, 0)),
                  _resident((1, D)),
                  _resident(w_in.shape),
                  _resident(p4.shape),
                  _resident(p16.shape)],
        out_specs=[pl.BlockSpec((None, tm, D_POOL), lambda b, i: (b, i, 0)),
                   nat_spec, nat_spec, nat_spec,
                   cls_spec(16), cls_spec(16), cls_spec(16),
                   cls_spec(4), cls_spec(4), cls_spec(4)],
        out_shape=[jax.ShapeDtypeStruct((B, S, D_POOL), _f32), nat, nat, nat,
                   cls_shape(16), cls_shape(16), cls_shape(16),
                   cls_shape(4), cls_shape(4), cls_shape(4)],
        compiler_params=_params(2),
        name="in_proj",
    )(h, mod3, g.reshape(1, D), w_in, p4, p16)


QB = 128


def _den_lane(h):
    return N_HEADS + h + (HEAD_DIM if h % 2 == 0 else 0)


def _attn_kernel(q_ref, km_ref, kh_ref, vm_ref, vh_ref, o_ref, st_ref, kbuf, vbuf, bias, *, dil, seq_len):
    b, i = pl.program_id(0), pl.program_id(1)
    tq = q_ref.shape[0]

    @pl.when((b == 0) & (i == 0))
    def _():
        row = jax.lax.broadcasted_iota(jnp.int32, (QB, 2 * QB), 0)
        col = jax.lax.broadcasted_iota(jnp.int32, (QB, 2 * QB), 1)
        delta = row - col + QB
        valid = (delta >= 0) & (delta <= WINDOW)
        dist = (delta * dil).astype(_f32)
        for h in range(N_HEADS):
            slope = 2.0 ** (-8.0 * (h + 1) / N_HEADS) * LOG2E
            bias[h] = jnp.where(valid, -slope * dist, NEG)

    kbuf[0:QB, :] = kh_ref[...]
    kbuf[QB:, :] = km_ref[0:QB, :]
    own_even = jax.lax.broadcasted_iota(jnp.int32, (1, D_ATTN), 1) % LANES < HEAD_DIM
    for dst, src in ((slice(0, QB), vh_ref), (slice(QB, None), vm_ref)):
        v = src[...]
        ones = jnp.ones_like(v)
        vbuf[0, dst, :] = jnp.where(own_even, v, ones)
        vbuf[1, dst, :] = jnp.where(own_even, ones, v)

    lane = jax.lax.broadcasted_iota(jnp.int32, (QB, LANES), 1)
    low_half = lane < HEAD_DIM
    prev_half = jax.lax.broadcasted_iota(jnp.int32, (1, 2 * QB), 1) < QB
    if seq_len >= tq:
        seq_start = {0: jnp.where(prev_half & (i % (seq_len // tq) == 0), NEG, 0.0)}
    else:
        seq_start = {j: jnp.where(prev_half, NEG, 0.0) for j in range(0, tq // QB, seq_len // QB)}

    for j in range(tq // QB):
        rows = slice(j * QB, (j + 1) * QB)
        stats = jnp.zeros((QB, LANES), _f32)
        for hp in range(N_HEADS // 2):
            cs = slice(hp * LANES, (hp + 1) * LANES)
            q2 = q_ref[rows, cs]
            k2 = kbuf[:, cs] if j == 0 else km_ref[(j - 1) * QB:(j + 1) * QB, cs]
            halves = []
            for e in range(2):
                h = 2 * hp + e
                qm = jnp.where(low_half if e == 0 else ~low_half, q2, jnp.zeros_like(q2))
                s = jax.lax.dot_general(qm, k2, (((1,), (1,)), ((), ())), preferred_element_type=_f32)
                s = s + bias[h]
                if j in seq_start:
                    s = s + seq_start[j]
                m = jnp.max(s, axis=-1, keepdims=True)
                p = jnp.exp2(s - m).astype(_bf16)
                pv = jnp.dot(p, vbuf[e, j * QB:(j + 2) * QB, cs], preferred_element_type=_f32)
                halves.append(pv)
                stats = jnp.where(lane == h, m, stats)
                stats = jnp.where(lane == _den_lane(h), pv, stats)
            o_ref[rows, cs] = jnp.where(low_half, halves[0], halves[1]).astype(_bf16)
        st_ref[rows, :] = stats


def _attention(q, k, v, dil):
    B, C, L, _ = q.shape
    S = C * L
    tq = ATTN_ROWS
    flat = [a.reshape(B, S, D_ATTN) for a in (q, k, v)]
    main = pl.BlockSpec((None, tq, D_ATTN), lambda b, i: (b, i, 0))
    halo = pl.BlockSpec((None, QB, D_ATTN), lambda b, i: (b, jnp.maximum(i * (tq // QB) - 1, 0), 0))
    num, stats = pl.pallas_call(
        functools.partial(_attn_kernel, dil=dil, seq_len=L),
        grid=(B, S // tq),
        in_specs=[main, main, halo, main, halo],
        out_specs=[main, pl.BlockSpec((None, tq, LANES), lambda b, i: (b, i, 0))],
        out_shape=[jax.ShapeDtypeStruct((B, S, D_ATTN), _bf16),
                   jax.ShapeDtypeStruct((B, S, LANES), _f32)],
        scratch_shapes=[pltpu.VMEM((2 * QB, D_ATTN), _bf16),
                        pltpu.VMEM((2, tq + QB, D_ATTN), _bf16),
                        pltpu.VMEM((N_HEADS, QB, 2 * QB), _f32)],
        compiler_params=_params(2),
        name=f"attn_d{dil}",
    )(flat[0], flat[1], flat[1], flat[2], flat[2])
    return num.reshape(B, C, L, D_ATTN), stats.reshape(B, C, L, LANES)


POOL_HALO = 16


def _gather_classes(ref, c, dil):
    per = PERM_ROWS // dil
    return jnp.concatenate([ref[cls, c * per:(c + 1) * per, :] for cls in range(dil)], axis=0)


def _unsort_num(o_ref, pt_ref, c, dil):
    return jnp.dot(pt_ref[...], _gather_classes(o_ref, c, dil), preferred_element_type=_f32)


def _unsort_stats(st_ref, pt_ref, c, dil):
    parts = jnp.concatenate(_split3(_gather_classes(st_ref, c, dil)), axis=1)
    y = jnp.dot(pt_ref[...], parts, preferred_element_type=_f32)
    return y[:, :LANES] + y[:, LANES:2 * LANES] + y[:, 2 * LANES:]


def _pool_chunk(ubuf, wp_ref, ps_ref, c, t0):
    ext = PERM_ROWS + POOL_HALO
    t = t0 + jax.lax.broadcasted_iota(jnp.int32, (PERM_ROWS, 1), 0)
    ys = []
    for g, w in enumerate(POOL_WINDOWS):
        cs = slice(g * POOL_GROUP_DIM, (g + 1) * POOL_GROUP_DIM)
        x = ubuf[c * PERM_ROWS:c * PERM_ROWS + ext, cs]
        win = x
        back = 1
        while back < w:
            win = win + pltpu.roll(win, back, 0)
            back *= 2
        count = jnp.minimum(t + 1, w).astype(_f32)
        tok = x[POOL_HALO:, :]
        pooled = (win[POOL_HALO:, :] / count - tok).astype(_bf16)
        y = jnp.dot(pooled, wp_ref[g], preferred_element_type=_f32) * ps_ref[:, cs]
        ys.append(y.astype(_bf16))
    return jnp.concatenate(ys, axis=1)


def _merge_chunk(o1_ref, s1_ref, o4_ref, s4_ref, o16_ref, s16_ref, p4t_ref, p16t_ref, c):
    rows = slice(c * PERM_ROWS, (c + 1) * PERM_ROWS)
    nums = (o1_ref[rows, :].astype(_f32), _unsort_num(o4_ref, p4t_ref, c, 4),
            _unsort_num(o16_ref, p16t_ref, c, 16))
    stats = (s1_ref[rows, :], _unsort_stats(s4_ref, p4t_ref, c, 4), _unsort_stats(s16_ref, p16t_ref, c, 16))
    top = jnp.maximum(jnp.maximum(stats[0], stats[1]), stats[2])
    es = [jnp.exp2(s - top) for s in stats]
    even_head = jax.lax.broadcasted_iota(jnp.int32, (PERM_ROWS, LANES), 1) % 2 == 0
    dens = [jnp.where(even_head, pltpu.roll(s, LANES - _den_lane(0), 1), pltpu.roll(s, LANES - _den_lane(1) + 1, 1))
            for s in stats]
    inv = 1.0 / (es[0] * dens[0] + es[1] * dens[1] + es[2] * dens[2])
    wts = [e * inv for e in es]
    upper = (jax.lax.broadcasted_iota(jnp.int32, (PERM_ROWS, LANES), 1) >= HEAD_DIM).astype(jnp.int32)
    ys = []
    for hp in range(N_HEADS // 2):
        cs = slice(hp * LANES, (hp + 1) * LANES)
        src_lane = 2 * hp + upper
        y = None
        for r in range(3):
            w = jnp.take_along_axis(wts[r], src_lane, axis=1, mode="promise_in_bounds")
            y = w * nums[r][:, cs] if y is None else y + w * nums[r][:, cs]
        ys.append(y.astype(_bf16))
    return jnp.concatenate(ys, axis=1)


def _mixout_kernel(o1_ref, s1_ref, o4_ref, s4_ref, o16_ref, s16_ref, u_ref, uh_ref, mod_ref,
                   wp_ref, ps_ref, wo_ref, p4t_ref, p16t_ref, out_ref, ubuf):
    i = pl.program_id(1)
    tm = out_ref.shape[0]
    ubuf[0:POOL_HALO, :] = jnp.where(i == 0, 0.0, uh_ref[...])
    ubuf[POOL_HALO:, :] = u_ref[...]
    gate = mod_ref[5:6, :]

    def mixed(c):
        y_pool = _pool_chunk(ubuf, wp_ref, ps_ref, c, i * tm + c * PERM_ROWS)
        y_attn = _merge_chunk(o1_ref, s1_ref, o4_ref, s4_ref, o16_ref, s16_ref, p4t_ref, p16t_ref, c)
        return jnp.concatenate([y_pool, y_attn], axis=1)

    n_chunks = tm // PERM_ROWS
    ready = mixed(0)
    for c in range(n_chunks):
        rows = slice(c * PERM_ROWS, (c + 1) * PERM_ROWS)
        cur = ready
        if c + 1 < n_chunks:
            ready = mixed(c + 1)
        y = jnp.dot(cur, wo_ref[...], preferred_element_type=_f32)
        out_ref[rows, :] = gate * y


def _mix_out(a1, a4, a16, u, mod3, w_pool, pool_scale, w_out, p4t, p16t):
    B, S = u.shape[:2]
    D = w_out.shape[1]
    tm = TM_MIX

    def branch_specs(dil, width):
        if dil == 1:
            return pl.BlockSpec((None, None, tm, width), lambda b, i: (b, 0, i, 0))
        return pl.BlockSpec((None, dil, tm // dil, width), lambda b, i: (b, 0, i, 0))

    return pl.pallas_call(
        _mixout_kernel,
        grid=(B, S // tm),
        in_specs=[branch_specs(1, D_ATTN), branch_specs(1, LANES),
                  branch_specs(4, D_ATTN), branch_specs(4, LANES),
                  branch_specs(16, D_ATTN), branch_specs(16, LANES),
                  pl.BlockSpec((None, tm, D_POOL), lambda b, i: (b, i, 0)),
                  pl.BlockSpec((None, POOL_HALO, D_POOL),
                               lambda b, i: (b, jnp.maximum(i * (tm // POOL_HALO) - 1, 0), 0)),
                  pl.BlockSpec((None, N_MOD, D), lambda b, i: (b, 0, 0)),
                  _resident(w_pool.shape),
                  _resident((1, D_POOL)),
                  _resident(w_out.shape),
                  _resident(p4t.shape),
                  _resident(p16t.shape)],
        out_specs=pl.BlockSpec((None, tm, D), lambda b, i: (b, i, 0)),
        out_shape=jax.ShapeDtypeStruct((B, S, D), _f32),
        scratch_shapes=[pltpu.VMEM((tm + POOL_HALO, D_POOL), _f32)],
        compiler_params=_params(2),
        name="mix_out",
    )(*a1, *a4, *a16, u, u, mod3, w_pool, pool_scale.reshape(1, D_POOL), w_out, p4t, p16t)


def kernel(x, c, w_ada, b_ada, g_ffn1, w1_gate, w1_up, w1_down, g_mix, w_in, w_pool, pool_scale, w_out,
           g_ffn2, w2_gate, w2_up, w2_down, g_final):
    B, S, D = x.shape
    depth = w_ada.shape[0]
    p4 = jnp.asarray(_sort_matrix(4), _bf16)
    p16 = jnp.asarray(_sort_matrix(16), _bf16)
    p4t = jnp.asarray(_sort_matrix(4).T, _bf16)
    p16t = jnp.asarray(_sort_matrix(16).T, _bf16)
    h = x
    for l in range(depth):
        mod, (wg1, wu1, wd1) = _ada_mod(c, w_ada[l], b_ada[l], (w1_gate[l], w1_up[l], w1_down[l]))
        mod3 = mod.reshape(B, N_MOD, D)
        pool_flat = w_pool[l].reshape(N_POOL_GROUPS * POOL_GROUP_DIM, POOL_GROUP_DIM)
        h, (wg2, wu2, wd2, w_in_b, w_out_b, w_pool_b) = _ffn(
            h, mod3, g_ffn1[l], wg1, wu1, wd1, g_final, mod_row=0, final_norm=False,
            narrow=(w2_gate[l], w2_up[l], w2_down[l], w_in[l], w_out[l], pool_flat))
        u, q1, k1, v1, q16, k16, v16, q4, k4, v4 = _in_proj(h, mod3, g_mix[l], w_in_b, p4, p16)
        a1 = _attention(q1[:, None], k1[:, None], v1[:, None], 1)
        a4 = _attention(q4, k4, v4, 4)
        a16 = _attention(q16, k16, v16, 16)
        mixed = _mix_out(a1, a4, a16, u, mod3, w_pool_b.reshape(w_pool[l].shape), pool_scale[l], w_out_b, p4t, p16t)
        h, _ = _ffn(h, mod3, g_ffn2[l], wg2, wu2, wd2, g_final, mod_row=6, final_norm=(l == depth - 1), delta=mixed)
    return h
```

```python
import functools
import math

import jax
import jax.numpy as jnp
from jax.experimental import pallas as pl
from jax.experimental.pallas import tpu as pltpu

D_MODEL = 1024
D_POOL = 512
D_ATTN = 512
POOL_WINDOWS = (2, 4, 8, 16)
POOL_GROUP_DIM = 128
N_POOL_GROUPS = len(POOL_WINDOWS)
HEAD_DIM = 64
N_HEADS = 8
WINDOW = 128
D_FF = 2816
N_MOD = 9
EPS = 1e-6
NEG = -1e30
LOG2E = math.log2(math.e)

LANES = 128
CHUNK = 256
VMEM_LIMIT = 56 * 1024 * 1024

TM_FFN = 1024
TM_PROJ = 1024
TM_MIX = 1024
ATTN_ROWS = 2048
BN_ADA = 1152

_f32 = jnp.float32
_bf16 = jnp.bfloat16


def _rms_mod(x, g, shift, scale):
    r = jax.lax.rsqrt(jnp.mean(x * x, axis=-1, keepdims=True) + EPS)
    return x * r * (g * (1.0 + scale)) + shift


def _resident(shape):
    nd = len(shape)
    return pl.BlockSpec(shape, lambda *_: (0,) * nd, pipeline_mode=pl.Buffered(1))


def _params(n_axes):
    return pltpu.CompilerParams(dimension_semantics=("arbitrary",) * n_axes, vmem_limit_bytes=VMEM_LIMIT)


def _ada_kernel(c_ref, w_ref, b_ref, *refs):
    n_cast = len(refs) // 2
    o_ref = refs[n_cast]
    c = c_ref[...]
    a = (c * (1.0 / (1.0 + jnp.exp(-c)))).astype(_bf16)
    o_ref[...] = jnp.dot(a, w_ref[...].astype(_bf16), preferred_element_type=_f32) + b_ref[...]
    for src, dst in zip(refs[:n_cast], refs[n_cast + 1:]):
        dst[...] = src[...].astype(_bf16)


def _ada_mod(c, w_ada, b_ada, narrow):
    B, D = c.shape
    N = w_ada.shape[1]
    steps = N // BN_ADA

    def cast_spec(w):
        return pl.BlockSpec((w.shape[0] // steps, w.shape[1]), lambda j: (j, 0))

    outs = pl.pallas_call(
        _ada_kernel,
        grid=(steps,),
        in_specs=[pl.BlockSpec((B, D), lambda j: (0, 0)),
                  pl.BlockSpec((D, BN_ADA), lambda j: (0, j)),
                  pl.BlockSpec((1, BN_ADA), lambda j: (0, j))] + [cast_spec(w) for w in narrow],
        out_specs=[pl.BlockSpec((B, BN_ADA), lambda j: (0, j))] + [cast_spec(w) for w in narrow],
        out_shape=[jax.ShapeDtypeStruct((B, N), _f32)] + [jax.ShapeDtypeStruct(w.shape, _bf16) for w in narrow],
        compiler_params=_params(1),
        name="ada_mod",
    )(c, w_ada, b_ada.reshape(1, N), *narrow)
    return outs[0], outs[1:]


FF_CHUNK = 256
N_FFN_IN = 7
N_CAST_BLOCKS = 16


def _ffn_kernel(*refs, mod_row, final_norm, has_delta):
    x_ref, mod_ref, g_ref, wg_ref, wu_ref, wd_ref, gf_ref = refs[:N_FFN_IN]
    n_in = N_FFN_IN + int(has_delta)
    n_cast = (len(refs) - n_in - 1) // 2
    cast_in = refs[n_in:n_in + n_cast]
    o_ref = refs[n_in + n_cast]
    cast_out = refs[n_in + n_cast + 1:]
    shift = mod_ref[mod_row:mod_row + 1, :]
    scale = mod_ref[mod_row + 1:mod_row + 2, :]
    gate = mod_ref[mod_row + 2:mod_row + 3, :]
    x = x_ref[...] + refs[N_FFN_IN][...] if has_delta else x_ref[...]
    n = _rms_mod(x, g_ref[...], shift, scale).astype(_bf16)
    for src, dst in zip(cast_in, cast_out):
        dst[...] = src[...].astype(_bf16)
    acc = jnp.zeros(x.shape, _f32)
    for c in range(D_FF // FF_CHUNK):
        cs = slice(c * FF_CHUNK, (c + 1) * FF_CHUNK)
        g = jnp.dot(n, wg_ref[:, cs], preferred_element_type=_f32)
        u = jnp.dot(n, wu_ref[:, cs], preferred_element_type=_f32)
        a = (g * (1.0 / (1.0 + jnp.exp(-g))) * u).astype(_bf16)
        acc = acc + jnp.dot(a, wd_ref[cs, :], preferred_element_type=_f32)
    h = x + (0.5 * gate) * acc
    if final_norm:
        h = h * jax.lax.rsqrt(jnp.mean(h * h, axis=-1, keepdims=True) + EPS) * gf_ref[...]
    o_ref[...] = h


def _ffn(x, mod3, g, wg, wu, wd, g_final, *, mod_row, final_norm, delta=None, narrow=()):
    B, S, D = x.shape
    tm = TM_FFN
    steps = B * (S // tm)
    per_b = S // tm

    def cast_spec(w):
        every = steps // N_CAST_BLOCKS
        return pl.BlockSpec((w.shape[0] // N_CAST_BLOCKS, w.shape[1]),
                            lambda b, i: ((b * per_b + i) // every, 0))

    row_tile = pl.BlockSpec((None, tm, D), lambda b, i: (b, i, 0))
    extra = () if delta is None else (delta,)
    kern = functools.partial(_ffn_kernel, mod_row=mod_row, final_norm=final_norm, has_delta=delta is not None)
    outs = pl.pallas_call(
        kern,
        grid=(B, per_b),
        in_specs=[row_tile,
                  pl.BlockSpec((None, N_MOD, D), lambda b, i: (b, 0, 0)),
                  _resident((1, D)),
                  _resident(wg.shape),
                  _resident(wu.shape),
                  _resident(wd.shape),
                  _resident((1, D))] + [row_tile] * len(extra) + [cast_spec(w) for w in narrow],
        out_specs=[row_tile] + [cast_spec(w) for w in narrow],
        out_shape=[jax.ShapeDtypeStruct((B, S, D), _f32)]
                  + [jax.ShapeDtypeStruct(w.shape, _bf16) for w in narrow],
        compiler_params=_params(2),
        name="ffn_final" if final_norm else "ffn",
    )(x, mod3, g.reshape(1, D), wg, wu, wd, g_final.reshape(1, D), *extra, *narrow)
    return outs[0], outs[1:]


N_SLAB = D_ATTN // LANES


def _inproj_kernel(h_ref, mod_ref, g_ref, w_ref,
                   u_ref, q1_ref, k1_ref, v1_ref, q4_ref, k4_ref, v4_ref, q16_ref, k16_ref, v16_ref, nbuf, nat, cls4):
    x = h_ref[...]
    tm = x.shape[0]
    n4, n16 = tm // 4, tm // 16
    nbuf[...] = _rms_mod(x, g_ref[...], mod_ref[3:4, :], mod_ref[4:5, :]).astype(_bf16)

    def project(col):
        return jnp.dot(nbuf[...], w_ref[:, col:col + D_ATTN], preferred_element_type=_f32)

    def emit(t, arr, o1, o4, o16):
        o1[...] = arr.astype(_bf16)
        for s in range(N_SLAB):
            slab = t * N_SLAB + s
            nat[slab] = arr[:, s * LANES:(s + 1) * LANES]
            for a in range(4):
                x4 = nat.at[slab][pl.ds(a, n4, stride=4), :]
                cls4[4 * slab + a] = x4
                o4[:, a * D_ATTN + s * LANES:a * D_ATTN + (s + 1) * LANES] = x4.astype(_bf16)
            for r in range(16):
                x16 = cls4.at[4 * slab + r % 4][pl.ds(r // 4, n16, stride=4), :]
                o16[:, r * D_ATTN + s * LANES:r * D_ATTN + (s + 1) * LANES] = x16.astype(_bf16)

    q = project(D_POOL) * (HEAD_DIM ** -0.5 * LOG2E)
    k = project(D_POOL + D_ATTN)
    emit(0, q, q1_ref, q4_ref, q16_ref)
    v = project(D_POOL + 2 * D_ATTN)
    emit(1, k, k1_ref, k4_ref, k16_ref)
    u = project(0)
    emit(2, v, v1_ref, v4_ref, v16_ref)
    u_ref[...] = u


def _in_proj(h, mod3, g, w_in):
    B, S, D = h.shape
    tm = TM_PROJ
    nat = jax.ShapeDtypeStruct((B, S, D_ATTN), _bf16)
    nat_spec = pl.BlockSpec((None, tm, D_ATTN), lambda b, i: (b, i, 0))

    def cls_shape(dil):
        return jax.ShapeDtypeStruct((B, S // dil, dil * D_ATTN), _bf16)

    def cls_spec(dil):
        return pl.BlockSpec((None, tm // dil, dil * D_ATTN), lambda b, i: (b, i, 0))

    return pl.pallas_call(
        _inproj_kernel,
        grid=(B, S // tm),
        in_specs=[pl.BlockSpec((None, tm, D), lambda b, i: (b, i, 0)),
                  pl.BlockSpec((None, N_MOD, D), lambda b, i: (b, 0, 0)),
                  _resident((1, D)),
                  _resident(w_in.shape)],
        out_specs=[pl.BlockSpec((None, tm, D_POOL), lambda b, i: (b, i, 0)),
                   nat_spec, nat_spec, nat_spec,
                   cls_spec(4), cls_spec(4), cls_spec(4),
                   cls_spec(16), cls_spec(16), cls_spec(16)],
        out_shape=[jax.ShapeDtypeStruct((B, S, D_POOL), _f32), nat, nat, nat,
                   cls_shape(4), cls_shape(4), cls_shape(4),
                   cls_shape(16), cls_shape(16), cls_shape(16)],
        scratch_shapes=[pltpu.VMEM((tm, D), _bf16),
                        pltpu.VMEM((3 * N_SLAB, tm, LANES), _f32),
                        pltpu.VMEM((3 * N_SLAB * 4, tm // 4, LANES), _f32)],
        compiler_params=_params(2),
        name="in_proj",
    )(h, mod3, g.reshape(1, D), w_in)


QB = 128


def _den_lane(h):
    return N_HEADS + h + (HEAD_DIM if h % 2 == 0 else 0)


def _attn_kernel(q_ref, km_ref, kh_ref, vm_ref, vh_ref, o_ref, st_ref, kbuf, vbuf, bias, *, dil):
    b, c, i = pl.program_id(0), pl.program_id(1), pl.program_id(2)
    tq = q_ref.shape[0]
    n_cls = q_ref.shape[1] // D_ATTN

    @pl.when((b == 0) & (c == 0) & (i == 0))
    def _():
        row = jax.lax.broadcasted_iota(jnp.int32, (QB, 2 * QB), 0)
        col = jax.lax.broadcasted_iota(jnp.int32, (QB, 2 * QB), 1)
        delta = row - col + QB
        valid = (delta >= 0) & (delta <= WINDOW)
        dist = (delta * dil).astype(_f32)
        for h in range(N_HEADS):
            slope = 2.0 ** (-8.0 * (h + 1) / N_HEADS) * LOG2E
            bias[h] = jnp.where(valid, -slope * dist, NEG)

    kbuf[0:QB, :] = kh_ref[...]
    kbuf[QB:, :] = km_ref[0:QB, :]
    own_even = jax.lax.broadcasted_iota(jnp.int32, (1, n_cls * D_ATTN), 1) % LANES < HEAD_DIM
    for dst, src in ((slice(0, QB), vh_ref), (slice(QB, None), vm_ref)):
        v = src[...]
        ones = jnp.ones_like(v)
        vbuf[0, dst, :] = jnp.where(own_even, v, ones)
        vbuf[1, dst, :] = jnp.where(own_even, ones, v)

    lane = jax.lax.broadcasted_iota(jnp.int32, (QB, LANES), 1)
    low_half = lane < HEAD_DIM
    prev_half = jax.lax.broadcasted_iota(jnp.int32, (1, 2 * QB), 1) < QB
    no_prev = jnp.where(prev_half & (i == 0), NEG, 0.0)

    for cls in range(n_cls):
        for j in range(tq // QB):
            rows = slice(j * QB, (j + 1) * QB)
            stats = jnp.zeros((QB, LANES), _f32)
            for hp in range(N_HEADS // 2):
                cs = slice(cls * D_ATTN + hp * LANES, cls * D_ATTN + (hp + 1) * LANES)
                q2 = q_ref[rows, cs]
                k2 = kbuf[:, cs] if j == 0 else km_ref[(j - 1) * QB:(j + 1) * QB, cs]
                halves = []
                for e in range(2):
                    h = 2 * hp + e
                    qm = jnp.where(low_half if e == 0 else ~low_half, q2, jnp.zeros_like(q2))
                    s = jax.lax.dot_general(qm, k2, (((1,), (1,)), ((), ())), preferred_element_type=_f32)
                    s = s + bias[h]
                    if j == 0:
                        s = s + no_prev
                    m = jnp.max(s, axis=-1, keepdims=True)
                    p = jnp.exp2(s - m).astype(_bf16)
                    pv = jnp.dot(p, vbuf[e, j * QB:(j + 2) * QB, cs], preferred_element_type=_f32)
                    halves.append(pv)
                    stats = jnp.where(lane == h, m, stats)
                    stats = jnp.where(lane == _den_lane(h), pv, stats)
                o_ref[rows, cs] = jnp.where(low_half, halves[0], halves[1]).astype(_bf16)
            st_ref[rows, cls * LANES:(cls + 1) * LANES] = stats


def _attention(q, k, v, dil):
    B, L, _ = q.shape
    tq = min(L, ATTN_ROWS)
    n_cls = min(dil, ATTN_ROWS // tq)

    def spec(rows, width, row_index):
        return pl.BlockSpec((None, rows, n_cls * width), lambda b, c, i: (b, row_index(i), c))

    main = spec(tq, D_ATTN, lambda i: i)
    halo = spec(QB, D_ATTN, lambda i: jnp.maximum(i * (tq // QB) - 1, 0))
    return pl.pallas_call(
        functools.partial(_attn_kernel, dil=dil),
        grid=(B, dil // n_cls, L // tq),
        in_specs=[main, main, halo, main, halo],
        out_specs=[main, spec(tq, LANES, lambda i: i)],
        out_shape=[jax.ShapeDtypeStruct((B, L, dil * D_ATTN), _bf16),
                   jax.ShapeDtypeStruct((B, L, dil * LANES), _f32)],
        scratch_shapes=[pltpu.VMEM((2 * QB, n_cls * D_ATTN), _bf16),
                        pltpu.VMEM((2, tq + QB, n_cls * D_ATTN), _bf16),
                        pltpu.VMEM((N_HEADS, QB, 2 * QB), _f32)],
        compiler_params=_params(3),
        name=f"attn_d{dil}",
    )(q, k, k, v, v)


POOL_HALO = 16
N_BRANCH_SLAB = N_SLAB + 1


def _to_natural(o4_ref, s4_ref, o16_ref, s16_ref, nat4, nat16, cls4):
    n4, n16 = nat4.shape[1] // 4, nat4.shape[1] // 16

    def slab_of(o_ref, s_ref, c, s):
        if s < N_SLAB:
            return o_ref[:, c * D_ATTN + s * LANES:c * D_ATTN + (s + 1) * LANES].astype(_f32)
        return s_ref[:, c * LANES:(c + 1) * LANES]

    for s in range(N_BRANCH_SLAB):
        for a in range(4):
            nat4.at[s][pl.ds(a, n4, stride=4), :] = slab_of(o4_ref, s4_ref, a, s)
        for r in range(16):
            cls4.at[N_BRANCH_SLAB * (r % 4) + s][pl.ds(r // 4, n16, stride=4), :] = slab_of(o16_ref, s16_ref, r, s)
    for s in range(N_BRANCH_SLAB):
        for a in range(4):
            nat16.at[s][pl.ds(a, n4, stride=4), :] = cls4[N_BRANCH_SLAB * a + s]


def _pool_chunk(ubuf, wp_ref, ps_ref, c, t0):
    ext = CHUNK + POOL_HALO
    t = t0 + jax.lax.broadcasted_iota(jnp.int32, (CHUNK, 1), 0)
    ys = []
    for g, w in enumerate(POOL_WINDOWS):
        cs = slice(g * POOL_GROUP_DIM, (g + 1) * POOL_GROUP_DIM)
        x = ubuf[c * CHUNK:c * CHUNK + ext, cs]
        win = x
        back = 1
        while back < w:
            win = win + pltpu.roll(win, back, 0)
            back *= 2
        count = jnp.minimum(t + 1, w).astype(_f32)
        tok = x[POOL_HALO:, :]
        pooled = (win[POOL_HALO:, :] / count - tok).astype(_bf16)
        y = jnp.dot(pooled, wp_ref[g], preferred_element_type=_f32) * ps_ref[:, cs]
        ys.append(y.astype(_bf16))
    return jnp.concatenate(ys, axis=1)


def _merge_chunk(o1_ref, s1_ref, nat4, nat16, c):
    rows = slice(c * CHUNK, (c + 1) * CHUNK)
    stats = (s1_ref[rows, :], nat4[N_SLAB, rows, :], nat16[N_SLAB, rows, :])
    top = jnp.maximum(jnp.maximum(stats[0], stats[1]), stats[2])
    es = [jnp.exp2(s - top) for s in stats]
    even_head = jax.lax.broadcasted_iota(jnp.int32, (CHUNK, LANES), 1) % 2 == 0
    dens = [jnp.where(even_head, pltpu.roll(s, LANES - _den_lane(0), 1), pltpu.roll(s, LANES - _den_lane(1) + 1, 1))
            for s in stats]
    inv = 1.0 / (es[0] * dens[0] + es[1] * dens[1] + es[2] * dens[2])
    wts = [e * inv for e in es]
    upper = (jax.lax.broadcasted_iota(jnp.int32, (CHUNK, LANES), 1) >= HEAD_DIM).astype(jnp.int32)
    ys = []
    for hp in range(N_HEADS // 2):
        nums = (o1_ref[rows, hp * LANES:(hp + 1) * LANES].astype(_f32), nat4[hp, rows, :], nat16[hp, rows, :])
        src_lane = 2 * hp + upper
        y = None
        for r in range(3):
            w = jnp.take_along_axis(wts[r], src_lane, axis=1, mode="promise_in_bounds")
            y = w * nums[r] if y is None else y + w * nums[r]
        ys.append(y.astype(_bf16))
    return jnp.concatenate(ys, axis=1)


def _mixout_kernel(o1_ref, s1_ref, o4_ref, s4_ref, o16_ref, s16_ref, u_ref, uh_ref, mod_ref,
                   wp_ref, ps_ref, wo_ref, out_ref, ubuf, nat4, nat16, cls4):
    i = pl.program_id(1)
    tm = out_ref.shape[0]
    ubuf[0:POOL_HALO, :] = jnp.where(i == 0, 0.0, uh_ref[...])
    ubuf[POOL_HALO:, :] = u_ref[...]
    _to_natural(o4_ref, s4_ref, o16_ref, s16_ref, nat4, nat16, cls4)
    gate = mod_ref[5:6, :]

    def mixed(c):
        y_pool = _pool_chunk(ubuf, wp_ref, ps_ref, c, i * tm + c * CHUNK)
        y_attn = _merge_chunk(o1_ref, s1_ref, nat4, nat16, c)
        return jnp.concatenate([y_pool, y_attn], axis=1)

    n_chunks = tm // CHUNK
    ready = mixed(0)
    for c in range(n_chunks):
        rows = slice(c * CHUNK, (c + 1) * CHUNK)
        cur = ready
        if c + 1 < n_chunks:
            ready = mixed(c + 1)
        y = jnp.dot(cur, wo_ref[...], preferred_element_type=_f32)
        out_ref[rows, :] = gate * y


def _mix_out(a1, a4, a16, u, mod3, w_pool, pool_scale, w_out):
    B, S = u.shape[:2]
    D = w_out.shape[1]
    tm = TM_MIX

    def branch_specs(dil, width):
        return pl.BlockSpec((None, tm // dil, dil * width), lambda b, i: (b, i, 0))

    return pl.pallas_call(
        _mixout_kernel,
        grid=(B, S // tm),
        in_specs=[branch_specs(1, D_ATTN), branch_specs(1, LANES),
                  branch_specs(4, D_ATTN), branch_specs(4, LANES),
                  branch_specs(16, D_ATTN), branch_specs(16, LANES),
                  pl.BlockSpec((None, tm, D_POOL), lambda b, i: (b, i, 0)),
                  pl.BlockSpec((None, POOL_HALO, D_POOL),
                               lambda b, i: (b, jnp.maximum(i * (tm // POOL_HALO) - 1, 0), 0)),
                  pl.BlockSpec((None, N_MOD, D), lambda b, i: (b, 0, 0)),
                  _resident(w_pool.shape),
                  _resident((1, D_POOL)),
                  _resident(w_out.shape)],
        out_specs=pl.BlockSpec((None, tm, D), lambda b, i: (b, i, 0)),
        out_shape=jax.ShapeDtypeStruct((B, S, D), _f32),
        scratch_shapes=[pltpu.VMEM((tm + POOL_HALO, D_POOL), _f32),
                        pltpu.VMEM((N_BRANCH_SLAB, tm, LANES), _f32),
                        pltpu.VMEM((N_BRANCH_SLAB, tm, LANES), _f32),
                        pltpu.VMEM((N_BRANCH_SLAB * 4, tm // 4, LANES), _f32)],
        compiler_params=_params(2),
        name="mix_out",
    )(*a1, *a4, *a16, u, u, mod3, w_pool, pool_scale.reshape(1, D_POOL), w_out)


def kernel(x, c, w_ada, b_ada, g_ffn1, w1_gate, w1_up, w1_down, g_mix, w_in, w_pool, pool_scale, w_out,
           g_ffn2, w2_gate, w2_up, w2_down, g_final):
    B, S, D = x.shape
    depth = w_ada.shape[0]
    h = x
    for l in range(depth):
        mod, (wg1, wu1, wd1) = _ada_mod(c, w_ada[l], b_ada[l], (w1_gate[l], w1_up[l], w1_down[l]))
        mod3 = mod.reshape(B, N_MOD, D)
        pool_flat = w_pool[l].reshape(N_POOL_GROUPS * POOL_GROUP_DIM, POOL_GROUP_DIM)
        h, (wg2, wu2, wd2, w_in_b, w_out_b, w_pool_b) = _ffn(
            h, mod3, g_ffn1[l], wg1, wu1, wd1, g_final, mod_row=0, final_norm=False,
            narrow=(w2_gate[l], w2_up[l], w2_down[l], w_in[l], w_out[l], pool_flat))
        u, q1, k1, v1, q4, k4, v4, q16, k16, v16 = _in_proj(h, mod3, g_mix[l], w_in_b)
        a1 = _attention(q1, k1, v1, 1)
        a4 = _attention(q4, k4, v4, 4)
        a16 = _attention(q16, k16, v16, 16)
        mixed = _mix_out(a1, a4, a16, u, mod3, w_pool_b.reshape(w_pool[l].shape), pool_scale[l], w_out_b)
        h, _ = _ffn(h, mod3, g_ffn2[l], wg2, wu2, wd2, g_final, mod_row=6, final_norm=(l == depth - 1), delta=mixed)
    return h
```

```python
import functools
import math

import jax
import jax.numpy as jnp
import numpy as np
from jax.experimental import pallas as pl
from jax.experimental.pallas import tpu as pltpu

D_MODEL = 1024
D_POOL = 512
D_ATTN = 512
POOL_WINDOWS = (2, 4, 8, 16)
POOL_GROUP_DIM = 128
N_POOL_GROUPS = len(POOL_WINDOWS)
HEAD_DIM = 64
N_HEADS = 8
WINDOW = 128
D_FF = 2816
N_MOD = 9
EPS = 1e-6
NEG = -1e30
LOG2E = math.log2(math.e)

LANES = 128
PERM_ROWS = 256
CHUNK = 256
VMEM_LIMIT = 56 * 1024 * 1024

TM_FFN = 1024
TM_PROJ = 1024
TM_MIX = 1024
ATTN_ROWS = 2048
BN_ADA = 1152

_f32 = jnp.float32
_bf16 = jnp.bfloat16


def _rms_mod(x, g, shift, scale):
    r = jax.lax.rsqrt(jnp.mean(x * x, axis=-1, keepdims=True) + EPS)
    return x * r * (g * (1.0 + scale)) + shift


def _resident(shape):
    nd = len(shape)
    return pl.BlockSpec(shape, lambda *_: (0,) * nd, pipeline_mode=pl.Buffered(1))


def _params(n_axes):
    return pltpu.CompilerParams(dimension_semantics=("arbitrary",) * n_axes, vmem_limit_bytes=VMEM_LIMIT)


def _ada_kernel(c_ref, w_ref, b_ref, *refs):
    n_cast = len(refs) // 2
    o_ref = refs[n_cast]
    c = c_ref[...]
    a = (c * (1.0 / (1.0 + jnp.exp(-c)))).astype(_bf16)
    o_ref[...] = jnp.dot(a, w_ref[...].astype(_bf16), preferred_element_type=_f32) + b_ref[...]
    for src, dst in zip(refs[:n_cast], refs[n_cast + 1:]):
        dst[...] = src[...].astype(_bf16)


def _ada_mod(c, w_ada, b_ada, narrow):
    B, D = c.shape
    N = w_ada.shape[1]
    steps = N // BN_ADA

    def cast_spec(w):
        return pl.BlockSpec((w.shape[0] // steps, w.shape[1]), lambda j: (j, 0))

    outs = pl.pallas_call(
        _ada_kernel,
        grid=(steps,),
        in_specs=[pl.BlockSpec((B, D), lambda j: (0, 0)),
                  pl.BlockSpec((D, BN_ADA), lambda j: (0, j)),
                  pl.BlockSpec((1, BN_ADA), lambda j: (0, j))] + [cast_spec(w) for w in narrow],
        out_specs=[pl.BlockSpec((B, BN_ADA), lambda j: (0, j))] + [cast_spec(w) for w in narrow],
        out_shape=[jax.ShapeDtypeStruct((B, N), _f32)] + [jax.ShapeDtypeStruct(w.shape, _bf16) for w in narrow],
        compiler_params=_params(1),
        name="ada_mod",
    )(c, w_ada, b_ada.reshape(1, N), *narrow)
    return outs[0], outs[1:]


FF_CHUNK = 256
N_FFN_IN = 7
N_CAST_BLOCKS = 16


def _ffn_kernel(*refs, mod_row, final_norm, has_delta):
    x_ref, mod_ref, g_ref, wg_ref, wu_ref, wd_ref, gf_ref = refs[:N_FFN_IN]
    n_in = N_FFN_IN + int(has_delta)
    n_cast = (len(refs) - n_in - 1) // 2
    cast_in = refs[n_in:n_in + n_cast]
    o_ref = refs[n_in + n_cast]
    cast_out = refs[n_in + n_cast + 1:]
    shift = mod_ref[mod_row:mod_row + 1, :]
    scale = mod_ref[mod_row + 1:mod_row + 2, :]
    gate = mod_ref[mod_row + 2:mod_row + 3, :]
    x = x_ref[...] + refs[N_FFN_IN][...] if has_delta else x_ref[...]
    n = _rms_mod(x, g_ref[...], shift, scale).astype(_bf16)
    for src, dst in zip(cast_in, cast_out):
        dst[...] = src[...].astype(_bf16)
    acc = jnp.zeros(x.shape, _f32)
    for c in range(D_FF // FF_CHUNK):
        cs = slice(c * FF_CHUNK, (c + 1) * FF_CHUNK)
        g = jnp.dot(n, wg_ref[:, cs], preferred_element_type=_f32)
        u = jnp.dot(n, wu_ref[:, cs], preferred_element_type=_f32)
        a = (g * (1.0 / (1.0 + jnp.exp(-g))) * u).astype(_bf16)
        acc = acc + jnp.dot(a, wd_ref[cs, :], preferred_element_type=_f32)
    h = x + (0.5 * gate) * acc
    if final_norm:
        h = h * jax.lax.rsqrt(jnp.mean(h * h, axis=-1, keepdims=True) + EPS) * gf_ref[...]
    o_ref[...] = h


def _ffn(x, mod3, g, wg, wu, wd, g_final, *, mod_row, final_norm, delta=None, narrow=()):
    B, S, D = x.shape
    tm = TM_FFN
    steps = B * (S // tm)
    per_b = S // tm

    def cast_spec(w):
        every = steps // N_CAST_BLOCKS
        return pl.BlockSpec((w.shape[0] // N_CAST_BLOCKS, w.shape[1]),
                            lambda b, i: ((b * per_b + i) // every, 0))

    row_tile = pl.BlockSpec((None, tm, D), lambda b, i: (b, i, 0))
    extra = () if delta is None else (delta,)
    kern = functools.partial(_ffn_kernel, mod_row=mod_row, final_norm=final_norm, has_delta=delta is not None)
    outs = pl.pallas_call(
        kern,
        grid=(B, per_b),
        in_specs=[row_tile,
                  pl.BlockSpec((None, N_MOD, D), lambda b, i: (b, 0, 0)),
                  _resident((1, D)),
                  _resident(wg.shape),
                  _resident(wu.shape),
                  _resident(wd.shape),
                  _resident((1, D))] + [row_tile] * len(extra) + [cast_spec(w) for w in narrow],
        out_specs=[row_tile] + [cast_spec(w) for w in narrow],
        out_shape=[jax.ShapeDtypeStruct((B, S, D), _f32)]
                  + [jax.ShapeDtypeStruct(w.shape, _bf16) for w in narrow],
        compiler_params=_params(2),
        name="ffn_final" if final_norm else "ffn",
    )(x, mod3, g.reshape(1, D), wg, wu, wd, g_final.reshape(1, D), *extra, *narrow)
    return outs[0], outs[1:]


def _sort_matrix(dil):
    per = PERM_ROWS // dil
    i = np.arange(PERM_ROWS)
    src = (i % per) * dil + i // per
    p = np.zeros((PERM_ROWS, PERM_ROWS), np.float32)
    p[i, src] = 1.0
    return p


def _inproj_kernel(h_ref, mod_ref, g_ref, w_ref, p4_ref, p16_ref,
                   u_ref, q1_ref, k1_ref, v1_ref, q4_ref, k4_ref, v4_ref, q16_ref, k16_ref, v16_ref):
    x = h_ref[...]
    tm = x.shape[0]
    n = _rms_mod(x, g_ref[...], mod_ref[3:4, :], mod_ref[4:5, :]).astype(_bf16)
    z = jnp.dot(n, w_ref[...], preferred_element_type=_f32)
    u_ref[...] = z[:, :D_POOL]
    q = (z[:, D_POOL:D_POOL + D_ATTN] * (HEAD_DIM ** -0.5 * LOG2E)).astype(_bf16)
    k = z[:, D_POOL + D_ATTN:D_POOL + 2 * D_ATTN].astype(_bf16)
    v = z[:, D_POOL + 2 * D_ATTN:].astype(_bf16)
    q1_ref[...] = q
    k1_ref[...] = k
    v1_ref[...] = v
    qkv = jnp.concatenate([q, k, v], axis=1)
    for dil, p_ref, outs in ((4, p4_ref, (q4_ref, k4_ref, v4_ref)),
                             (16, p16_ref, (q16_ref, k16_ref, v16_ref))):
        per = PERM_ROWS // dil
        for c in range(tm // PERM_ROWS):
            s = jnp.dot(p_ref[...], qkv[c * PERM_ROWS:(c + 1) * PERM_ROWS, :],
                        preferred_element_type=_f32).astype(_bf16)
            for a, o_ref in enumerate(outs):
                for cls in range(dil):
                    o_ref[cls, c * per:(c + 1) * per, :] = s[cls * per:(cls + 1) * per,
                                                             a * D_ATTN:(a + 1) * D_ATTN]


def _in_proj(h, mod3, g, w_in, p4, p16):
    B, S, D = h.shape
    tm = TM_PROJ
    nat = jax.ShapeDtypeStruct((B, S, D_ATTN), _bf16)
    nat_spec = pl.BlockSpec((None, tm, D_ATTN), lambda b, i: (b, i, 0))

    def cls_shape(dil):
        return jax.ShapeDtypeStruct((B, dil, S // dil, D_ATTN), _bf16)

    def cls_spec(dil):
        return pl.BlockSpec((None, dil, tm // dil, D_ATTN), lambda b, i: (b, 0, i, 0))

    return pl.pallas_call(
        _inproj_kernel,
        grid=(B, S // tm),
        in_specs=[pl.BlockSpec((None, tm, D), lambda b, i: (b, i, 0)),
                  pl.BlockSpec((None, N_MOD, D), lambda b, i: (b, 0, 0)),
                  _resident((1, D)),
                  _resident(w_in.shape),
                  _resident(p4.shape),
                  _resident(p16.shape)],
        out_specs=[pl.BlockSpec((None, tm, D_POOL), lambda b, i: (b, i, 0)),
                   nat_spec, nat_spec, nat_spec,
                   cls_spec(4), cls_spec(4), cls_spec(4),
                   cls_spec(16), cls_spec(16), cls_spec(16)],
        out_shape=[jax.ShapeDtypeStruct((B, S, D_POOL), _f32), nat, nat, nat,
                   cls_shape(4), cls_shape(4), cls_shape(4),
                   cls_shape(16), cls_shape(16), cls_shape(16)],
        compiler_params=_params(2),
        name="in_proj",
    )(h, mod3, g.reshape(1, D), w_in, p4, p16)


QB = 128


def _den_lane(h):
    return N_HEADS + h + (HEAD_DIM if h % 2 == 0 else 0)


def _attn_kernel(q_ref, km_ref, kh_ref, vm_ref, vh_ref, o_ref, st_ref, kbuf, vbuf, bias, *, dil, seq_len):
    b, i = pl.program_id(0), pl.program_id(1)
    tq = q_ref.shape[0]

    @pl.when((b == 0) & (i == 0))
    def _():
        row = jax.lax.broadcasted_iota(jnp.int32, (QB, 2 * QB), 0)
        col = jax.lax.broadcasted_iota(jnp.int32, (QB, 2 * QB), 1)
        delta = row - col + QB
        valid = (delta >= 0) & (delta <= WINDOW)
        dist = (delta * dil).astype(_f32)
        for h in range(N_HEADS):
            slope = 2.0 ** (-8.0 * (h + 1) / N_HEADS) * LOG2E
            bias[h] = jnp.where(valid, -slope * dist, NEG)

    kbuf[0:QB, :] = kh_ref[...]
    kbuf[QB:, :] = km_ref[0:QB, :]
    own_even = jax.lax.broadcasted_iota(jnp.int32, (1, D_ATTN), 1) % LANES < HEAD_DIM
    for dst, src in ((slice(0, QB), vh_ref), (slice(QB, None), vm_ref)):
        v = src[...]
        ones = jnp.ones_like(v)
        vbuf[0, dst, :] = jnp.where(own_even, v, ones)
        vbuf[1, dst, :] = jnp.where(own_even, ones, v)

    lane = jax.lax.broadcasted_iota(jnp.int32, (QB, LANES), 1)
    low_half = lane < HEAD_DIM
    prev_half = jax.lax.broadcasted_iota(jnp.int32, (1, 2 * QB), 1) < QB
    if seq_len >= tq:
        seq_start = {0: jnp.where(prev_half & (i % (seq_len // tq) == 0), NEG, 0.0)}
    else:
        seq_start = {j: jnp.where(prev_half, NEG, 0.0) for j in range(0, tq // QB, seq_len // QB)}

    for j in range(tq // QB):
        rows = slice(j * QB, (j + 1) * QB)
        stats = jnp.zeros((QB, LANES), _f32)
        for hp in range(N_HEADS // 2):
            cs = slice(hp * LANES, (hp + 1) * LANES)
            q2 = q_ref[rows, cs]
            k2 = kbuf[:, cs] if j == 0 else km_ref[(j - 1) * QB:(j + 1) * QB, cs]
            halves = []
            for e in range(2):
                h = 2 * hp + e
                qm = jnp.where(low_half if e == 0 else ~low_half, q2, jnp.zeros_like(q2))
                s = jax.lax.dot_general(qm, k2, (((1,), (1,)), ((), ())), preferred_element_type=_f32)
                s = s + bias[h]
                if j in seq_start:
                    s = s + seq_start[j]
                m = jnp.max(s, axis=-1, keepdims=True)
                p = jnp.exp2(s - m).astype(_bf16)
                pv = jnp.dot(p, vbuf[e, j * QB:(j + 2) * QB, cs], preferred_element_type=_f32)
                halves.append(pv)
                stats = jnp.where(lane == h, m, stats)
                stats = jnp.where(lane == _den_lane(h), pv, stats)
            o_ref[rows, cs] = jnp.where(low_half, halves[0], halves[1]).astype(_bf16)
        st_ref[rows, :] = stats


def _attention(q, k, v, dil, tq=ATTN_ROWS):
    B, C, L, _ = q.shape
    S = C * L
    flat = [a.reshape(B, S, D_ATTN) for a in (q, k, v)]
    main = pl.BlockSpec((None, tq, D_ATTN), lambda b, i: (b, i, 0))
    halo = pl.BlockSpec((None, QB, D_ATTN), lambda b, i: (b, jnp.maximum(i * (tq // QB) - 1, 0), 0))
    num, stats = pl.pallas_call(
        functools.partial(_attn_kernel, dil=dil, seq_len=L),
        grid=(B, S // tq),
        in_specs=[main, main, halo, main, halo],
        out_specs=[main, pl.BlockSpec((None, tq, LANES), lambda b, i: (b, i, 0))],
        out_shape=[jax.ShapeDtypeStruct((B, S, D_ATTN), _bf16),
                   jax.ShapeDtypeStruct((B, S, LANES), _f32)],
        scratch_shapes=[pltpu.VMEM((2 * QB, D_ATTN), _bf16),
                        pltpu.VMEM((2, tq + QB, D_ATTN), _bf16),
                        pltpu.VMEM((N_HEADS, QB, 2 * QB), _f32)],
        compiler_params=_params(2),
        name=f"attn_d{dil}",
    )(flat[0], flat[1], flat[1], flat[2], flat[2])
    return num.reshape(B, C, L, D_ATTN), stats.reshape(B, C, L, LANES)


POOL_HALO = 16
N_SLAB = D_ATTN // LANES
N_BRANCH_SLAB = N_SLAB + 1


def _to_natural(o4_ref, s4_ref, o16_ref, s16_ref, nat4, nat16, cls4):
    n4, n16 = nat4.shape[1] // 4, nat4.shape[1] // 16

    def slab_of(o_ref, s_ref, c, s):
        if s < N_SLAB:
            return o_ref[c, :, s * LANES:(s + 1) * LANES].astype(_f32)
        return s_ref[c]

    for s in range(N_BRANCH_SLAB):
        for a in range(4):
            nat4.at[s][pl.ds(a, n4, stride=4), :] = slab_of(o4_ref, s4_ref, a, s)
        for r in range(16):
            cls4.at[N_BRANCH_SLAB * (r % 4) + s][pl.ds(r // 4, n16, stride=4), :] = slab_of(o16_ref, s16_ref, r, s)
    for s in range(N_BRANCH_SLAB):
        for a in range(4):
            nat16.at[s][pl.ds(a, n4, stride=4), :] = cls4[N_BRANCH_SLAB * a + s]


def _pool_chunk(ubuf, wp_ref, ps_ref, c, t0):
    ext = CHUNK + POOL_HALO
    t = t0 + jax.lax.broadcasted_iota(jnp.int32, (CHUNK, 1), 0)
    ys = []
    for g, w in enumerate(POOL_WINDOWS):
        cs = slice(g * POOL_GROUP_DIM, (g + 1) * POOL_GROUP_DIM)
        x = ubuf[c * CHUNK:c * CHUNK + ext, cs]
        win = x
        back = 1
        while back < w:
            win = win + pltpu.roll(win, back, 0)
            back *= 2
        count = jnp.minimum(t + 1, w).astype(_f32)
        tok = x[POOL_HALO:, :]
        pooled = (win[POOL_HALO:, :] / count - tok).astype(_bf16)
        y = jnp.dot(pooled, wp_ref[g], preferred_element_type=_f32) * ps_ref[:, cs]
        ys.append(y.astype(_bf16))
    return jnp.concatenate(ys, axis=1)


def _merge_chunk(o1_ref, s1_ref, nat4, nat16, c):
    rows = slice(c * CHUNK, (c + 1) * CHUNK)
    stats = (s1_ref[rows, :], nat4[N_SLAB, rows, :], nat16[N_SLAB, rows, :])
    top = jnp.maximum(jnp.maximum(stats[0], stats[1]), stats[2])
    es = [jnp.exp2(s - top) for s in stats]
    even_head = jax.lax.broadcasted_iota(jnp.int32, (CHUNK, LANES), 1) % 2 == 0
    dens = [jnp.where(even_head, pltpu.roll(s, LANES - _den_lane(0), 1), pltpu.roll(s, LANES - _den_lane(1) + 1, 1))
            for s in stats]
    inv = 1.0 / (es[0] * dens[0] + es[1] * dens[1] + es[2] * dens[2])
    wts = [e * inv for e in es]
    upper = (jax.lax.broadcasted_iota(jnp.int32, (CHUNK, LANES), 1) >= HEAD_DIM).astype(jnp.int32)
    ys = []
    for hp in range(N_HEADS // 2):
        nums = (o1_ref[rows, hp * LANES:(hp + 1) * LANES].astype(_f32), nat4[hp, rows, :], nat16[hp, rows, :])
        src_lane = 2 * hp + upper
        y = None
        for r in range(3):
            w = jnp.take_along_axis(wts[r], src_lane, axis=1, mode="promise_in_bounds")
            y = w * nums[r] if y is None else y + w * nums[r]
        ys.append(y.astype(_bf16))
    return jnp.concatenate(ys, axis=1)


def _mixout_kernel(o1_ref, s1_ref, o4_ref, s4_ref, o16_ref, s16_ref, u_ref, uh_ref, mod_ref,
                   wp_ref, ps_ref, wo_ref, out_ref, ubuf, nat4, nat16, cls4):
    i = pl.program_id(1)
    tm = out_ref.shape[0]
    ubuf[0:POOL_HALO, :] = jnp.where(i == 0, 0.0, uh_ref[...])
    ubuf[POOL_HALO:, :] = u_ref[...]
    _to_natural(o4_ref, s4_ref, o16_ref, s16_ref, nat4, nat16, cls4)
    gate = mod_ref[5:6, :]

    def mixed(c):
        y_pool = _pool_chunk(ubuf, wp_ref, ps_ref, c, i * tm + c * CHUNK)
        y_attn = _merge_chunk(o1_ref, s1_ref, nat4, nat16, c)
        return jnp.concatenate([y_pool, y_attn], axis=1)

    n_chunks = tm // CHUNK
    ready = mixed(0)
    for c in range(n_chunks):
        rows = slice(c * CHUNK, (c + 1) * CHUNK)
        cur = ready
        if c + 1 < n_chunks:
            ready = mixed(c + 1)
        y = jnp.dot(cur, wo_ref[...], preferred_element_type=_f32)
        out_ref[rows, :] = gate * y


def _mix_out(a1, a4, a16, u, mod3, w_pool, pool_scale, w_out):
    B, S = u.shape[:2]
    D = w_out.shape[1]
    tm = TM_MIX

    def branch_specs(dil, width):
        if dil == 1:
            return pl.BlockSpec((None, None, tm, width), lambda b, i: (b, 0, i, 0))
        return pl.BlockSpec((None, dil, tm // dil, width), lambda b, i: (b, 0, i, 0))

    return pl.pallas_call(
        _mixout_kernel,
        grid=(B, S // tm),
        in_specs=[branch_specs(1, D_ATTN), branch_specs(1, LANES),
                  branch_specs(4, D_ATTN), branch_specs(4, LANES),
                  branch_specs(16, D_ATTN), branch_specs(16, LANES),
                  pl.BlockSpec((None, tm, D_POOL), lambda b, i: (b, i, 0)),
                  pl.BlockSpec((None, POOL_HALO, D_POOL),
                               lambda b, i: (b, jnp.maximum(i * (tm // POOL_HALO) - 1, 0), 0)),
                  pl.BlockSpec((None, N_MOD, D), lambda b, i: (b, 0, 0)),
                  _resident(w_pool.shape),
                  _resident((1, D_POOL)),
                  _resident(w_out.shape)],
        out_specs=pl.BlockSpec((None, tm, D), lambda b, i: (b, i, 0)),
        out_shape=jax.ShapeDtypeStruct((B, S, D), _f32),
        scratch_shapes=[pltpu.VMEM((tm + POOL_HALO, D_POOL), _f32),
                        pltpu.VMEM((N_BRANCH_SLAB, tm, LANES), _f32),
                        pltpu.VMEM((N_BRANCH_SLAB, tm, LANES), _f32),
                        pltpu.VMEM((N_BRANCH_SLAB * 4, tm // 4, LANES), _f32)],
        compiler_params=_params(2),
        name="mix_out",
    )(*a1, *a4, *a16, u, u, mod3, w_pool, pool_scale.reshape(1, D_POOL), w_out)


def kernel(x, c, w_ada, b_ada, g_ffn1, w1_gate, w1_up, w1_down, g_mix, w_in, w_pool, pool_scale, w_out,
           g_ffn2, w2_gate, w2_up, w2_down, g_final):
    B, S, D = x.shape
    depth = w_ada.shape[0]
    p4 = jnp.asarray(_sort_matrix(4), _bf16)
    p16 = jnp.asarray(_sort_matrix(16), _bf16)
    h = x
    for l in range(depth):
        mod, (wg1, wu1, wd1) = _ada_mod(c, w_ada[l], b_ada[l], (w1_gate[l], w1_up[l], w1_down[l]))
        mod3 = mod.reshape(B, N_MOD, D)
        pool_flat = w_pool[l].reshape(N_POOL_GROUPS * POOL_GROUP_DIM, POOL_GROUP_DIM)
        h, (wg2, wu2, wd2, w_in_b, w_out_b, w_pool_b) = _ffn(
            h, mod3, g_ffn1[l], wg1, wu1, wd1, g_final, mod_row=0, final_norm=False,
            narrow=(w2_gate[l], w2_up[l], w2_down[l], w_in[l], w_out[l], pool_flat))
        u, q1, k1, v1, q4, k4, v4, q16, k16, v16 = _in_proj(h, mod3, g_mix[l], w_in_b, p4, p16)
        a1 = _attention(q1[:, None], k1[:, None], v1[:, None], 1)
        a4 = _attention(q4, k4, v4, 4, tq=ATTN_ROWS // 2)
        a16 = _attention(q16, k16, v16, 16)
        mixed = _mix_out(a1, a4, a16, u, mod3, w_pool_b.reshape(w_pool[l].shape), pool_scale[l], w_out_b)
        h, _ = _ffn(h, mod3, g_ffn2[l], wg2, wu2, wd2, g_final, mod_row=6, final_norm=(l == depth - 1), delta=mixed)
    return h
```

```python
import functools
import math

import jax
import jax.numpy as jnp
import numpy as np
from jax.experimental import pallas as pl
from jax.experimental.pallas import tpu as pltpu

D_MODEL = 1024
D_POOL = 512
D_ATTN = 512
POOL_WINDOWS = (2, 4, 8, 16)
POOL_GROUP_DIM = 128
N_POOL_GROUPS = len(POOL_WINDOWS)
HEAD_DIM = 64
N_HEADS = 8
WINDOW = 128
D_FF = 2816
N_MOD = 9
EPS = 1e-6
NEG = -1e30
LOG2E = math.log2(math.e)

LANES = 128
PERM_ROWS = 256
CHUNK = 256
VMEM_LIMIT = 56 * 1024 * 1024

TM_FFN = 1024
TM_PROJ = 1024
TM_MIXFFN = 512
ATTN_ROWS = 2048
BN_ADA = 1152

_f32 = jnp.float32
_bf16 = jnp.bfloat16


def _rms_mod(x, g, shift, scale):
    r = jax.lax.rsqrt(jnp.mean(x * x, axis=-1, keepdims=True) + EPS)
    return x * r * (g * (1.0 + scale)) + shift


def _resident(shape):
    nd = len(shape)
    return pl.BlockSpec(shape, lambda *_: (0,) * nd, pipeline_mode=pl.Buffered(1))


def _params(n_axes):
    return pltpu.CompilerParams(dimension_semantics=("arbitrary",) * n_axes, vmem_limit_bytes=VMEM_LIMIT)


def _ada_kernel(c_ref, w_ref, b_ref, *refs):
    n_cast = len(refs) // 2
    o_ref = refs[n_cast]
    c = c_ref[...]
    a = (c * (1.0 / (1.0 + jnp.exp(-c)))).astype(_bf16)
    o_ref[...] = jnp.dot(a, w_ref[...].astype(_bf16), preferred_element_type=_f32) + b_ref[...]
    for src, dst in zip(refs[:n_cast], refs[n_cast + 1:]):
        dst[...] = src[...].astype(_bf16)


def _ada_mod(c, w_ada, b_ada, narrow):
    B, D = c.shape
    N = w_ada.shape[1]
    steps = N // BN_ADA

    def cast_spec(w):
        return pl.BlockSpec((w.shape[0] // steps, w.shape[1]), lambda j: (j, 0))

    outs = pl.pallas_call(
        _ada_kernel,
        grid=(steps,),
        in_specs=[pl.BlockSpec((B, D), lambda j: (0, 0)),
                  pl.BlockSpec((D, BN_ADA), lambda j: (0, j)),
                  pl.BlockSpec((1, BN_ADA), lambda j: (0, j))] + [cast_spec(w) for w in narrow],
        out_specs=[pl.BlockSpec((B, BN_ADA), lambda j: (0, j))] + [cast_spec(w) for w in narrow],
        out_shape=[jax.ShapeDtypeStruct((B, N), _f32)] + [jax.ShapeDtypeStruct(w.shape, _bf16) for w in narrow],
        compiler_params=_params(1),
        name="ada_mod",
    )(c, w_ada, b_ada.reshape(1, N), *narrow)
    return outs[0], outs[1:]


FF_CHUNK = 256
N_FFN_IN = 7
N_CAST_BLOCKS = 16


def _ffn_kernel(*refs, mod_row, final_norm, has_delta):
    x_ref, mod_ref, g_ref, wg_ref, wu_ref, wd_ref, gf_ref = refs[:N_FFN_IN]
    n_in = N_FFN_IN + int(has_delta)
    n_cast = (len(refs) - n_in - 1) // 2
    cast_in = refs[n_in:n_in + n_cast]
    o_ref = refs[n_in + n_cast]
    cast_out = refs[n_in + n_cast + 1:]
    shift = mod_ref[mod_row:mod_row + 1, :]
    scale = mod_ref[mod_row + 1:mod_row + 2, :]
    gate = mod_ref[mod_row + 2:mod_row + 3, :]
    x = x_ref[...] + refs[N_FFN_IN][...] if has_delta else x_ref[...]
    n = _rms_mod(x, g_ref[...], shift, scale).astype(_bf16)
    for src, dst in zip(cast_in, cast_out):
        dst[...] = src[...].astype(_bf16)
    acc = jnp.zeros(x.shape, _f32)
    for c in range(D_FF // FF_CHUNK):
        cs = slice(c * FF_CHUNK, (c + 1) * FF_CHUNK)
        g = jnp.dot(n, wg_ref[:, cs], preferred_element_type=_f32)
        u = jnp.dot(n, wu_ref[:, cs], preferred_element_type=_f32)
        a = (g * (1.0 / (1.0 + jnp.exp(-g))) * u).astype(_bf16)
        acc = acc + jnp.dot(a, wd_ref[cs, :], preferred_element_type=_f32)
    h = x + (0.5 * gate) * acc
    if final_norm:
        h = h * jax.lax.rsqrt(jnp.mean(h * h, axis=-1, keepdims=True) + EPS) * gf_ref[...]
    o_ref[...] = h


def _ffn(x, mod3, g, wg, wu, wd, g_final, *, mod_row, final_norm, delta=None, narrow=()):
    B, S, D = x.shape
    tm = TM_FFN
    steps = B * (S // tm)
    per_b = S // tm

    def cast_spec(w):
        every = steps // N_CAST_BLOCKS
        return pl.BlockSpec((w.shape[0] // N_CAST_BLOCKS, w.shape[1]),
                            lambda b, i: ((b * per_b + i) // every, 0))

    row_tile = pl.BlockSpec((None, tm, D), lambda b, i: (b, i, 0))
    extra = () if delta is None else (delta,)
    kern = functools.partial(_ffn_kernel, mod_row=mod_row, final_norm=final_norm, has_delta=delta is not None)
    outs = pl.pallas_call(
        kern,
        grid=(B, per_b),
        in_specs=[row_tile,
                  pl.BlockSpec((None, N_MOD, D), lambda b, i: (b, 0, 0)),
                  _resident((1, D)),
                  _resident(wg.shape),
                  _resident(wu.shape),
                  _resident(wd.shape),
                  _resident((1, D))] + [row_tile] * len(extra) + [cast_spec(w) for w in narrow],
        out_specs=[row_tile] + [cast_spec(w) for w in narrow],
        out_shape=[jax.ShapeDtypeStruct((B, S, D), _f32)]
                  + [jax.ShapeDtypeStruct(w.shape, _bf16) for w in narrow],
        compiler_params=_params(2),
        name="ffn_final" if final_norm else "ffn",
    )(x, mod3, g.reshape(1, D), wg, wu, wd, g_final.reshape(1, D), *extra, *narrow)
    return outs[0], outs[1:]


def _sort_matrix(dil):
    per = PERM_ROWS // dil
    i = np.arange(PERM_ROWS)
    src = (i % per) * dil + i // per
    p = np.zeros((PERM_ROWS, PERM_ROWS), np.float32)
    p[i, src] = 1.0
    return p


def _inproj_kernel(h_ref, mod_ref, g_ref, w_ref, p4_ref, p16_ref,
                   u_ref, q1_ref, k1_ref, v1_ref, q4_ref, k4_ref, v4_ref, q16_ref, k16_ref, v16_ref):
    x = h_ref[...]
    tm = x.shape[0]
    n = _rms_mod(x, g_ref[...], mod_ref[3:4, :], mod_ref[4:5, :]).astype(_bf16)
    z = jnp.dot(n, w_ref[...], preferred_element_type=_f32)
    u_ref[...] = z[:, :D_POOL]
    q = (z[:, D_POOL:D_POOL + D_ATTN] * (HEAD_DIM ** -0.5 * LOG2E)).astype(_bf16)
    k = z[:, D_POOL + D_ATTN:D_POOL + 2 * D_ATTN].astype(_bf16)
    v = z[:, D_POOL + 2 * D_ATTN:].astype(_bf16)
    q1_ref[...] = q
    k1_ref[...] = k
    v1_ref[...] = v
    qkv = jnp.concatenate([q, k, v], axis=1)
    for dil, p_ref, outs in ((4, p4_ref, (q4_ref, k4_ref, v4_ref)),
                             (16, p16_ref, (q16_ref, k16_ref, v16_ref))):
        per = PERM_ROWS // dil
        for c in range(tm // PERM_ROWS):
            s = jnp.dot(p_ref[...], qkv[c * PERM_ROWS:(c + 1) * PERM_ROWS, :],
                        preferred_element_type=_f32).astype(_bf16)
            for a, o_ref in enumerate(outs):
                for cls in range(dil):
                    o_ref[cls, c * per:(c + 1) * per, :] = s[cls * per:(cls + 1) * per,
                                                             a * D_ATTN:(a + 1) * D_ATTN]


def _in_proj(h, mod3, g, w_in, p4, p16):
    B, S, D = h.shape
    tm = TM_PROJ
    nat = jax.ShapeDtypeStruct((B, S, D_ATTN), _bf16)
    nat_spec = pl.BlockSpec((None, tm, D_ATTN), lambda b, i: (b, i, 0))

    def cls_shape(dil):
        return jax.ShapeDtypeStruct((B, dil, S // dil, D_ATTN), _bf16)

    def cls_spec(dil):
        return pl.BlockSpec((None, dil, tm // dil, D_ATTN), lambda b, i: (b, 0, i, 0))

    return pl.pallas_call(
        _inproj_kernel,
        grid=(B, S // tm),
        in_specs=[pl.BlockSpec((None, tm, D), lambda b, i: (b, i, 0)),
                  pl.BlockSpec((None, N_MOD, D), lambda b, i: (b, 0, 0)),
                  _resident((1, D)),
                  _resident(w_in.shape),
                  _resident(p4.shape),
                  _resident(p16.shape)],
        out_specs=[pl.BlockSpec((None, tm, D_POOL), lambda b, i: (b, i, 0)),
                   nat_spec, nat_spec, nat_spec,
                   cls_spec(4), cls_spec(4), cls_spec(4),
                   cls_spec(16), cls_spec(16), cls_spec(16)],
        out_shape=[jax.ShapeDtypeStruct((B, S, D_POOL), _f32), nat, nat, nat,
                   cls_shape(4), cls_shape(4), cls_shape(4),
                   cls_shape(16), cls_shape(16), cls_shape(16)],
        compiler_params=_params(2),
        name="in_proj",
    )(h, mod3, g.reshape(1, D), w_in, p4, p16)


QB = 128


def _den_lane(h):
    return N_HEADS + h + (HEAD_DIM if h % 2 == 0 else 0)


def _attn_kernel(q_ref, km_ref, kh_ref, vm_ref, vh_ref, o_ref, st_ref, kbuf, vbuf, bias, *, dil, seq_len):
    b, i = pl.program_id(0), pl.program_id(1)
    tq = q_ref.shape[0]

    @pl.when((b == 0) & (i == 0))
    def _():
        row = jax.lax.broadcasted_iota(jnp.int32, (QB, 2 * QB), 0)
        col = jax.lax.broadcasted_iota(jnp.int32, (QB, 2 * QB), 1)
        delta = row - col + QB
        valid = (delta >= 0) & (delta <= WINDOW)
        dist = (delta * dil).astype(_f32)
        for h in range(N_HEADS):
            slope = 2.0 ** (-8.0 * (h + 1) / N_HEADS) * LOG2E
            bias[h] = jnp.where(valid, -slope * dist, NEG)

    kbuf[0:QB, :] = kh_ref[...]
    kbuf[QB:, :] = km_ref[0:QB, :]
    own_even = jax.lax.broadcasted_iota(jnp.int32, (1, D_ATTN), 1) % LANES < HEAD_DIM
    for dst, src in ((slice(0, QB), vh_ref), (slice(QB, None), vm_ref)):
        v = src[...]
        ones = jnp.ones_like(v)
        vbuf[0, dst, :] = jnp.where(own_even, v, ones)
        vbuf[1, dst, :] = jnp.where(own_even, ones, v)

    lane = jax.lax.broadcasted_iota(jnp.int32, (QB, LANES), 1)
    low_half = lane < HEAD_DIM
    prev_half = jax.lax.broadcasted_iota(jnp.int32, (1, 2 * QB), 1) < QB
    if seq_len >= tq:
        seq_start = {0: jnp.where(prev_half & (i % (seq_len // tq) == 0), NEG, 0.0)}
    else:
        seq_start = {j: jnp.where(prev_half, NEG, 0.0) for j in range(0, tq // QB, seq_len // QB)}

    for j in range(tq // QB):
        rows = slice(j * QB, (j + 1) * QB)
        stats = jnp.zeros((QB, LANES), _f32)
        for hp in range(N_HEADS // 2):
            cs = slice(hp * LANES, (hp + 1) * LANES)
            q2 = q_ref[rows, cs]
            k2 = kbuf[:, cs] if j == 0 else km_ref[(j - 1) * QB:(j + 1) * QB, cs]
            halves = []
            for e in range(2):
                h = 2 * hp + e
                qm = jnp.where(low_half if e == 0 else ~low_half, q2, jnp.zeros_like(q2))
                s = jax.lax.dot_general(qm, k2, (((1,), (1,)), ((), ())), preferred_element_type=_f32)
                s = s + bias[h]
                if j in seq_start:
                    s = s + seq_start[j]
                m = jnp.max(s, axis=-1, keepdims=True)
                p = jnp.exp2(s - m).astype(_bf16)
                pv = jnp.dot(p, vbuf[e, j * QB:(j + 2) * QB, cs], preferred_element_type=_f32)
                halves.append(pv)
                stats = jnp.where(lane == h, m, stats)
                stats = jnp.where(lane == _den_lane(h), pv, stats)
            o_ref[rows, cs] = jnp.where(low_half, halves[0], halves[1]).astype(_bf16)
        st_ref[rows, :] = stats


def _attention(q, k, v, dil, tq=ATTN_ROWS):
    B, C, L, _ = q.shape
    S = C * L
    flat = [a.reshape(B, S, D_ATTN) for a in (q, k, v)]
    main = pl.BlockSpec((None, tq, D_ATTN), lambda b, i: (b, i, 0))
    halo = pl.BlockSpec((None, QB, D_ATTN), lambda b, i: (b, jnp.maximum(i * (tq // QB) - 1, 0), 0))
    num, stats = pl.pallas_call(
        functools.partial(_attn_kernel, dil=dil, seq_len=L),
        grid=(B, S // tq),
        in_specs=[main, main, halo, main, halo],
        out_specs=[main, pl.BlockSpec((None, tq, LANES), lambda b, i: (b, i, 0))],
        out_shape=[jax.ShapeDtypeStruct((B, S, D_ATTN), _bf16),
                   jax.ShapeDtypeStruct((B, S, LANES), _f32)],
        scratch_shapes=[pltpu.VMEM((2 * QB, D_ATTN), _bf16),
                        pltpu.VMEM((2, tq + QB, D_ATTN), _bf16),
                        pltpu.VMEM((N_HEADS, QB, 2 * QB), _f32)],
        compiler_params=_params(2),
        name=f"attn_d{dil}",
    )(flat[0], flat[1], flat[1], flat[2], flat[2])
    return num.reshape(B, C, L, D_ATTN), stats.reshape(B, C, L, LANES)


POOL_HALO = 16
N_SLAB = D_ATTN // LANES
N_BRANCH_SLAB = N_SLAB + 1


def _to_natural(o4_ref, s4_ref, o16_ref, s16_ref, nat4, nat16, cls4):
    n4, n16 = nat4.shape[1] // 4, nat4.shape[1] // 16

    def slab_of(o_ref, s_ref, c, s):
        if s < N_SLAB:
            return o_ref[c, :, s * LANES:(s + 1) * LANES].astype(_f32)
        return s_ref[c]

    for s in range(N_BRANCH_SLAB):
        for a in range(4):
            nat4.at[s][pl.ds(a, n4, stride=4), :] = slab_of(o4_ref, s4_ref, a, s)
        for r in range(16):
            cls4.at[N_BRANCH_SLAB * (r % 4) + s][pl.ds(r // 4, n16, stride=4), :] = slab_of(o16_ref, s16_ref, r, s)
    for s in range(N_BRANCH_SLAB):
        for a in range(4):
            nat16.at[s][pl.ds(a, n4, stride=4), :] = cls4[N_BRANCH_SLAB * a + s]


def _pool_chunk(ubuf, wp_ref, ps_ref, c, t0):
    ext = CHUNK + POOL_HALO
    t = t0 + jax.lax.broadcasted_iota(jnp.int32, (CHUNK, 1), 0)
    ys = []
    for g, w in enumerate(POOL_WINDOWS):
        cs = slice(g * POOL_GROUP_DIM, (g + 1) * POOL_GROUP_DIM)
        x = ubuf[c * CHUNK:c * CHUNK + ext, cs]
        win = x
        back = 1
        while back < w:
            win = win + pltpu.roll(win, back, 0)
            back *= 2
        count = jnp.minimum(t + 1, w).astype(_f32)
        tok = x[POOL_HALO:, :]
        pooled = (win[POOL_HALO:, :] / count - tok).astype(_bf16)
        y = jnp.dot(pooled, wp_ref[g], preferred_element_type=_f32) * ps_ref[:, cs]
        ys.append(y.astype(_bf16))
    return jnp.concatenate(ys, axis=1)


def _merge_chunk(o1_ref, s1_ref, nat4, nat16, c):
    rows = slice(c * CHUNK, (c + 1) * CHUNK)
    stats = (s1_ref[rows, :], nat4[N_SLAB, rows, :], nat16[N_SLAB, rows, :])
    top = jnp.maximum(jnp.maximum(stats[0], stats[1]), stats[2])
    es = [jnp.exp2(s - top) for s in stats]
    even_head = jax.lax.broadcasted_iota(jnp.int32, (CHUNK, LANES), 1) % 2 == 0
    dens = [jnp.where(even_head, pltpu.roll(s, LANES - _den_lane(0), 1), pltpu.roll(s, LANES - _den_lane(1) + 1, 1))
            for s in stats]
    inv = 1.0 / (es[0] * dens[0] + es[1] * dens[1] + es[2] * dens[2])
    wts = [e * inv for e in es]
    upper = (jax.lax.broadcasted_iota(jnp.int32, (CHUNK, LANES), 1) >= HEAD_DIM).astype(jnp.int32)
    ys = []
    for hp in range(N_HEADS // 2):
        nums = (o1_ref[rows, hp * LANES:(hp + 1) * LANES].astype(_f32), nat4[hp, rows, :], nat16[hp, rows, :])
        src_lane = 2 * hp + upper
        y = None
        for r in range(3):
            w = jnp.take_along_axis(wts[r], src_lane, axis=1, mode="promise_in_bounds")
            y = w * nums[r] if y is None else y + w * nums[r]
        ys.append(y.astype(_bf16))
    return jnp.concatenate(ys, axis=1)


def _mixffn_kernel(o1_ref, s1_ref, o4_ref, s4_ref, o16_ref, s16_ref, u_ref, uh_ref, modn_ref, wp_ref, ps_ref, wo_ref,
                   hn_ref, h_ref, mod_ref, g_ref, wg_ref, wu_ref, wd_ref, gf_ref, out_ref,
                   delta, nbuf, ubuf, nat4, nat16, cls4, *, final_norm, per_b):
    s = pl.program_id(0)
    tm = h_ref.shape[0]
    last = pl.num_programs(0) - 2
    i_next = jnp.minimum(s, last) % per_b
    gate_mix = modn_ref[5:6, :]

    def load_mixer_inputs():
        ubuf[0:POOL_HALO, :] = jnp.where(i_next == 0, 0.0, uh_ref[...])
        ubuf[POOL_HALO:, :] = u_ref[...]
        _to_natural(o4_ref, s4_ref, o16_ref, s16_ref, nat4, nat16, cls4)

    def mixed(c):
        y_pool = _pool_chunk(ubuf, wp_ref, ps_ref, c, i_next * tm + c * CHUNK)
        y_attn = _merge_chunk(o1_ref, s1_ref, nat4, nat16, c)
        return jnp.concatenate([y_pool, y_attn], axis=1)

    def project(c, y, slot):
        delta[slot, c * CHUNK:(c + 1) * CHUNK, :] = gate_mix * jnp.dot(y, wo_ref[...], preferred_element_type=_f32)

    def normalise(slot):
        x_next = hn_ref[...] + delta[slot]
        nbuf[slot] = _rms_mod(x_next, g_ref[...], modn_ref[6:7, :], modn_ref[7:8, :]).astype(_bf16)

    n_chunks = tm // CHUNK

    @pl.when(s == 0)
    def _():
        load_mixer_inputs()
        for c in range(n_chunks):
            project(c, mixed(c), 0)
        normalise(0)

    @pl.when(s > 0)
    def _():
        n = nbuf[(s - 1) % 2]
        pending = {}
        after_chunk = {0: load_mixer_inputs, 2 * n_chunks + 3: functools.partial(normalise, s % 2)}
        for c in range(n_chunks):
            after_chunk[1 + 2 * c] = functools.partial(lambda c: pending.__setitem__(c, mixed(c)), c)
            after_chunk[2 + 2 * c] = functools.partial(lambda c: project(c, pending[c], s % 2), c)
        acc = jnp.zeros((tm, h_ref.shape[1]), _f32)
        for c in range(D_FF // FF_CHUNK):
            cs = slice(c * FF_CHUNK, (c + 1) * FF_CHUNK)
            g = jnp.dot(n, wg_ref[:, cs], preferred_element_type=_f32)
            u = jnp.dot(n, wu_ref[:, cs], preferred_element_type=_f32)
            a = (g * (1.0 / (1.0 + jnp.exp(-g))) * u).astype(_bf16)
            acc = acc + jnp.dot(a, wd_ref[cs, :], preferred_element_type=_f32)
            if c in after_chunk:
                after_chunk[c]()
        h = h_ref[...] + delta[(s - 1) % 2] + (0.5 * mod_ref[8:9, :]) * acc
        if final_norm:
            h = h * jax.lax.rsqrt(jnp.mean(h * h, axis=-1, keepdims=True) + EPS) * gf_ref[...]
        out_ref[...] = h


def _mix_ffn(a1, a4, a16, u, h, mod3, w_pool, pool_scale, w_out, g, wg, wu, wd, g_final, *, final_norm):
    B, S, D = h.shape
    tm = TM_MIXFFN
    per_b = S // tm
    n_tiles = B * per_b
    assert tm % CHUNK == 0

    def tile(t):
        return t // per_b, t % per_b

    def nxt(s):
        return tile(jnp.minimum(s, n_tiles - 1))

    def cur(s):
        return tile(jnp.maximum(s - 1, 0))

    def branch_spec(dil, width):
        if dil == 1:
            return pl.BlockSpec((None, None, tm, width), lambda s: (nxt(s)[0], 0, nxt(s)[1], 0))
        return pl.BlockSpec((None, dil, tm // dil, width), lambda s: (nxt(s)[0], 0, nxt(s)[1], 0))

    row_tile = pl.BlockSpec((None, tm, D), lambda s: (cur(s)[0], cur(s)[1], 0))
    return pl.pallas_call(
        functools.partial(_mixffn_kernel, final_norm=final_norm, per_b=per_b),
        grid=(n_tiles + 1,),
        in_specs=[branch_spec(1, D_ATTN), branch_spec(1, LANES),
                  branch_spec(4, D_ATTN), branch_spec(4, LANES),
                  branch_spec(16, D_ATTN), branch_spec(16, LANES),
                  pl.BlockSpec((None, tm, D_POOL), lambda s: (nxt(s)[0], nxt(s)[1], 0)),
                  pl.BlockSpec((None, POOL_HALO, D_POOL),
                               lambda s: (nxt(s)[0], jnp.maximum(nxt(s)[1] * (tm // POOL_HALO) - 1, 0), 0)),
                  pl.BlockSpec((None, N_MOD, D), lambda s: (nxt(s)[0], 0, 0)),
                  _resident(w_pool.shape),
                  _resident((1, D_POOL)),
                  _resident(w_out.shape),
                  pl.BlockSpec((None, tm, D), lambda s: (nxt(s)[0], nxt(s)[1], 0)),
                  row_tile,
                  pl.BlockSpec((None, N_MOD, D), lambda s: (cur(s)[0], 0, 0)),
                  _resident((1, D)),
                  _resident(wg.shape),
                  _resident(wu.shape),
                  _resident(wd.shape),
                  _resident((1, D))],
        out_specs=row_tile,
        out_shape=jax.ShapeDtypeStruct((B, S, D), _f32),
        scratch_shapes=[pltpu.VMEM((2, tm, D), _f32),
                        pltpu.VMEM((2, tm, D), _bf16),
                        pltpu.VMEM((tm + POOL_HALO, D_POOL), _f32),
                        pltpu.VMEM((N_BRANCH_SLAB, tm, LANES), _f32),
                        pltpu.VMEM((N_BRANCH_SLAB, tm, LANES), _f32),
                        pltpu.VMEM((N_BRANCH_SLAB * 4, tm // 4, LANES), _f32)],
        compiler_params=_params(1),
        name="mix_ffn",
    )(*a1, *a4, *a16, u, u, mod3, w_pool, pool_scale.reshape(1, D_POOL), w_out,
      h, h, mod3, g.reshape(1, D), wg, wu, wd, g_final.reshape(1, D))


def kernel(x, c, w_ada, b_ada, g_ffn1, w1_gate, w1_up, w1_down, g_mix, w_in, w_pool, pool_scale, w_out,
           g_ffn2, w2_gate, w2_up, w2_down, g_final):
    B, S, D = x.shape
    depth = w_ada.shape[0]
    p4 = jnp.asarray(_sort_matrix(4), _bf16)
    p16 = jnp.asarray(_sort_matrix(16), _bf16)
    h = x
    for l in range(depth):
        mod, (wg1, wu1, wd1) = _ada_mod(c, w_ada[l], b_ada[l], (w1_gate[l], w1_up[l], w1_down[l]))
        mod3 = mod.reshape(B, N_MOD, D)
        pool_flat = w_pool[l].reshape(N_POOL_GROUPS * POOL_GROUP_DIM, POOL_GROUP_DIM)
        h, (wg2, wu2, wd2, w_in_b, w_out_b, w_pool_b) = _ffn(
            h, mod3, g_ffn1[l], wg1, wu1, wd1, g_final, mod_row=0, final_norm=False,
            narrow=(w2_gate[l], w2_up[l], w2_down[l], w_in[l], w_out[l], pool_flat))
        u, q1, k1, v1, q4, k4, v4, q16, k16, v16 = _in_proj(h, mod3, g_mix[l], w_in_b, p4, p16)
        a1 = _attention(q1[:, None], k1[:, None], v1[:, None], 1)
        a4 = _attention(q4, k4, v4, 4, tq=ATTN_ROWS // 2)
        a16 = _attention(q16, k16, v16, 16)
        h = _mix_ffn(a1, a4, a16, u, h, mod3, w_pool_b.reshape(w_pool[l].shape), pool_scale[l], w_out_b,
                     g_ffn2[l], wg2, wu2, wd2, g_final, final_norm=(l == depth - 1))
    return h
```

```python
import functools
import math

import jax
import jax.numpy as jnp
import numpy as np
from jax.experimental import pallas as pl
from jax.experimental.pallas import tpu as pltpu

D_MODEL = 1024
D_POOL = 512
D_ATTN = 512
POOL_WINDOWS = (2, 4, 8, 16)
POOL_GROUP_DIM = 128
N_POOL_GROUPS = len(POOL_WINDOWS)
HEAD_DIM = 64
N_HEADS = 8
WINDOW = 128
D_FF = 2816
N_MOD = 9
EPS = 1e-6
NEG = -1e30
LOG2E = math.log2(math.e)

LANES = 128
PERM_ROWS = 256
CHUNK = 256
VMEM_LIMIT = 56 * 1024 * 1024

TM_FFN = 1024
TM_PROJ = 1024
TM_MIXFFN = 512
ATTN_ROWS = 2048
BN_ADA = 1152

_f32 = jnp.float32
_bf16 = jnp.bfloat16


def _rms_mod(x, g, shift, scale):
    r = jax.lax.rsqrt(jnp.mean(x * x, axis=-1, keepdims=True) + EPS)
    return x * r * (g * (1.0 + scale)) + shift


def _resident(shape):
    nd = len(shape)
    return pl.BlockSpec(shape, lambda *_: (0,) * nd, pipeline_mode=pl.Buffered(1))


def _params(n_axes):
    return pltpu.CompilerParams(dimension_semantics=("arbitrary",) * n_axes, vmem_limit_bytes=VMEM_LIMIT)


def _ada_kernel(c_ref, w_ref, b_ref, *refs):
    n_cast = len(refs) // 2
    o_ref = refs[n_cast]
    c = c_ref[...]
    a = (c * (1.0 / (1.0 + jnp.exp(-c)))).astype(_bf16)
    o_ref[...] = jnp.dot(a, w_ref[...].astype(_bf16), preferred_element_type=_f32) + b_ref[...]
    for src, dst in zip(refs[:n_cast], refs[n_cast + 1:]):
        dst[...] = src[...].astype(_bf16)


def _ada_mod(c, w_ada, b_ada, narrow):
    B, D = c.shape
    N = w_ada.shape[1]
    steps = N // BN_ADA

    def cast_spec(w):
        return pl.BlockSpec((w.shape[0] // steps, w.shape[1]), lambda j: (j, 0))

    outs = pl.pallas_call(
        _ada_kernel,
        grid=(steps,),
        in_specs=[pl.BlockSpec((B, D), lambda j: (0, 0)),
                  pl.BlockSpec((D, BN_ADA), lambda j: (0, j)),
                  pl.BlockSpec((1, BN_ADA), lambda j: (0, j))] + [cast_spec(w) for w in narrow],
        out_specs=[pl.BlockSpec((B, BN_ADA), lambda j: (0, j))] + [cast_spec(w) for w in narrow],
        out_shape=[jax.ShapeDtypeStruct((B, N), _f32)] + [jax.ShapeDtypeStruct(w.shape, _bf16) for w in narrow],
        compiler_params=_params(1),
        name="ada_mod",
    )(c, w_ada, b_ada.reshape(1, N), *narrow)
    return outs[0], outs[1:]


FF_CHUNK = 256
N_CAST_BLOCKS = 16


def _swiglu(n, wg_ref, wu_ref, wd_ref, after_chunk=None):
    acc = jnp.zeros((n.shape[0], wd_ref.shape[1]), _f32)
    for c in range(D_FF // FF_CHUNK):
        cs = slice(c * FF_CHUNK, (c + 1) * FF_CHUNK)
        g = jnp.dot(n, wg_ref[:, cs], preferred_element_type=_f32)
        u = jnp.dot(n, wu_ref[:, cs], preferred_element_type=_f32)
        a = (g * (1.0 / (1.0 + jnp.exp(-g))) * u).astype(_bf16)
        acc = acc + jnp.dot(a, wd_ref[cs, :], preferred_element_type=_f32)
        if after_chunk and c in after_chunk:
            after_chunk[c]()
    return acc


def _ffn_kernel(x_ref, mod_ref, g_ref, wg_ref, wu_ref, wd_ref, *refs):
    n_cast = len(refs) // 2
    o_ref = refs[n_cast]
    x = x_ref[...]
    n = _rms_mod(x, g_ref[...], mod_ref[0:1, :], mod_ref[1:2, :]).astype(_bf16)
    for src, dst in zip(refs[:n_cast], refs[n_cast + 1:]):
        dst[...] = src[...].astype(_bf16)
    o_ref[...] = x + (0.5 * mod_ref[2:3, :]) * _swiglu(n, wg_ref, wu_ref, wd_ref)


def _ffn(x, mod3, g, wg, wu, wd, narrow):
    B, S, D = x.shape
    tm = TM_FFN
    steps = B * (S // tm)
    per_b = S // tm

    def cast_spec(w):
        every = steps // N_CAST_BLOCKS
        return pl.BlockSpec((w.shape[0] // N_CAST_BLOCKS, w.shape[1]),
                            lambda b, i: ((b * per_b + i) // every, 0))

    row_tile = pl.BlockSpec((None, tm, D), lambda b, i: (b, i, 0))
    outs = pl.pallas_call(
        _ffn_kernel,
        grid=(B, per_b),
        in_specs=[row_tile,
                  pl.BlockSpec((None, N_MOD, D), lambda b, i: (b, 0, 0)),
                  _resident((1, D)),
                  _resident(wg.shape),
                  _resident(wu.shape),
                  _resident(wd.shape)] + [cast_spec(w) for w in narrow],
        out_specs=[row_tile] + [cast_spec(w) for w in narrow],
        out_shape=[jax.ShapeDtypeStruct((B, S, D), _f32)]
                  + [jax.ShapeDtypeStruct(w.shape, _bf16) for w in narrow],
        compiler_params=_params(2),
        name="ffn",
    )(x, mod3, g.reshape(1, D), wg, wu, wd, *narrow)
    return outs[0], outs[1:]


def _sort_matrix(dil):
    per = PERM_ROWS // dil
    i = np.arange(PERM_ROWS)
    src = (i % per) * dil + i // per
    p = np.zeros((PERM_ROWS, PERM_ROWS), np.float32)
    p[i, src] = 1.0
    return p


def _inproj_kernel(h_ref, mod_ref, g_ref, w_ref, p4_ref, p16_ref,
                   u_ref, q1_ref, k1_ref, v1_ref, q4_ref, k4_ref, v4_ref, q16_ref, k16_ref, v16_ref):
    x = h_ref[...]
    tm = x.shape[0]
    n = _rms_mod(x, g_ref[...], mod_ref[3:4, :], mod_ref[4:5, :]).astype(_bf16)
    z = jnp.dot(n, w_ref[...], preferred_element_type=_f32)
    u_ref[...] = z[:, :D_POOL]
    q = (z[:, D_POOL:D_POOL + D_ATTN] * (HEAD_DIM ** -0.5 * LOG2E)).astype(_bf16)
    k = z[:, D_POOL + D_ATTN:D_POOL + 2 * D_ATTN].astype(_bf16)
    v = z[:, D_POOL + 2 * D_ATTN:].astype(_bf16)
    q1_ref[...] = q
    k1_ref[...] = k
    v1_ref[...] = v
    qkv = jnp.concatenate([q, k, v], axis=1)
    for dil, p_ref, outs in ((4, p4_ref, (q4_ref, k4_ref, v4_ref)),
                             (16, p16_ref, (q16_ref, k16_ref, v16_ref))):
        per = PERM_ROWS // dil
        for c in range(tm // PERM_ROWS):
            s = jnp.dot(p_ref[...], qkv[c * PERM_ROWS:(c + 1) * PERM_ROWS, :],
                        preferred_element_type=_f32).astype(_bf16)
            for a, o_ref in enumerate(outs):
                for cls in range(dil):
                    o_ref[cls, c * per:(c + 1) * per, :] = s[cls * per:(cls + 1) * per,
                                                             a * D_ATTN:(a + 1) * D_ATTN]


def _in_proj(h, mod3, g, w_in, p4, p16):
    B, S, D = h.shape
    tm = TM_PROJ
    nat = jax.ShapeDtypeStruct((B, S, D_ATTN), _bf16)
    nat_spec = pl.BlockSpec((None, tm, D_ATTN), lambda b, i: (b, i, 0))

    def cls_shape(dil):
        return jax.ShapeDtypeStruct((B, dil, S // dil, D_ATTN), _bf16)

    def cls_spec(dil):
        return pl.BlockSpec((None, dil, tm // dil, D_ATTN), lambda b, i: (b, 0, i, 0))

    return pl.pallas_call(
        _inproj_kernel,
        grid=(B, S // tm),
        in_specs=[pl.BlockSpec((None, tm, D), lambda b, i: (b, i, 0)),
                  pl.BlockSpec((None, N_MOD, D), lambda b, i: (b, 0, 0)),
                  _resident((1, D)),
                  _resident(w_in.shape),
                  _resident(p4.shape),
                  _resident(p16.shape)],
        out_specs=[pl.BlockSpec((None, tm, D_POOL), lambda b, i: (b, i, 0)),
                   nat_spec, nat_spec, nat_spec,
                   cls_spec(4), cls_spec(4), cls_spec(4),
                   cls_spec(16), cls_spec(16), cls_spec(16)],
        out_shape=[jax.ShapeDtypeStruct((B, S, D_POOL), _f32), nat, nat, nat,
                   cls_shape(4), cls_shape(4), cls_shape(4),
                   cls_shape(16), cls_shape(16), cls_shape(16)],
        compiler_params=_params(2),
        name="in_proj",
    )(h, mod3, g.reshape(1, D), w_in, p4, p16)


QB = 128


def _den_lane(h):
    return N_HEADS + h + (HEAD_DIM if h % 2 == 0 else 0)


def _attn_kernel(q_ref, km_ref, kh_ref, vm_ref, vh_ref, o_ref, st_ref, kbuf, vbuf, bias, *, dil, seq_len):
    b, i = pl.program_id(0), pl.program_id(1)
    tq = q_ref.shape[0]

    @pl.when((b == 0) & (i == 0))
    def _():
        row = jax.lax.broadcasted_iota(jnp.int32, (QB, 2 * QB), 0)
        col = jax.lax.broadcasted_iota(jnp.int32, (QB, 2 * QB), 1)
        delta = row - col + QB
        valid = (delta >= 0) & (delta <= WINDOW)
        dist = (delta * dil).astype(_f32)
        for h in range(N_HEADS):
            slope = 2.0 ** (-8.0 * (h + 1) / N_HEADS) * LOG2E
            bias[h] = jnp.where(valid, -slope * dist, NEG)

    kbuf[0:QB, :] = kh_ref[...]
    kbuf[QB:, :] = km_ref[0:QB, :]
    own_even = jax.lax.broadcasted_iota(jnp.int32, (1, D_ATTN), 1) % LANES < HEAD_DIM
    for dst, src in ((slice(0, QB), vh_ref), (slice(QB, None), vm_ref)):
        v = src[...]
        ones = jnp.ones_like(v)
        vbuf[0, dst, :] = jnp.where(own_even, v, ones)
        vbuf[1, dst, :] = jnp.where(own_even, ones, v)

    lane = jax.lax.broadcasted_iota(jnp.int32, (QB, LANES), 1)
    low_half = lane < HEAD_DIM
    prev_half = jax.lax.broadcasted_iota(jnp.int32, (1, 2 * QB), 1) < QB
    if seq_len >= tq:
        seq_start = {0: jnp.where(prev_half & (i % (seq_len // tq) == 0), NEG, 0.0)}
    else:
        seq_start = {j: jnp.where(prev_half, NEG, 0.0) for j in range(0, tq // QB, seq_len // QB)}

    for j in range(tq // QB):
        rows = slice(j * QB, (j + 1) * QB)
        stats = jnp.zeros((QB, LANES), _f32)
        for hp in range(N_HEADS // 2):
            cs = slice(hp * LANES, (hp + 1) * LANES)
            q2 = q_ref[rows, cs]
            k2 = kbuf[:, cs] if j == 0 else km_ref[(j - 1) * QB:(j + 1) * QB, cs]
            halves = []
            for e in range(2):
                h = 2 * hp + e
                qm = jnp.where(low_half if e == 0 else ~low_half, q2, jnp.zeros_like(q2))
                s = jax.lax.dot_general(qm, k2, (((1,), (1,)), ((), ())), preferred_element_type=_f32)
                s = s + bias[h]
                if j in seq_start:
                    s = s + seq_start[j]
                m = jnp.max(s, axis=-1, keepdims=True)
                p = jnp.exp2(s - m).astype(_bf16)
                pv = jnp.dot(p, vbuf[e, j * QB:(j + 2) * QB, cs], preferred_element_type=_f32)
                halves.append(pv)
                stats = jnp.where(lane == h, m, stats)
                stats = jnp.where(lane == _den_lane(h), pv, stats)
            o_ref[rows, cs] = jnp.where(low_half, halves[0], halves[1]).astype(_bf16)
        st_ref[rows, :] = stats


def _attention(q, k, v, dil, tq=ATTN_ROWS):
    B, C, L, _ = q.shape
    S = C * L
    flat = [a.reshape(B, S, D_ATTN) for a in (q, k, v)]
    main = pl.BlockSpec((None, tq, D_ATTN), lambda b, i: (b, i, 0))
    halo = pl.BlockSpec((None, QB, D_ATTN), lambda b, i: (b, jnp.maximum(i * (tq // QB) - 1, 0), 0))
    num, stats = pl.pallas_call(
        functools.partial(_attn_kernel, dil=dil, seq_len=L),
        grid=(B, S // tq),
        in_specs=[main, main, halo, main, halo],
        out_specs=[main, pl.BlockSpec((None, tq, LANES), lambda b, i: (b, i, 0))],
        out_shape=[jax.ShapeDtypeStruct((B, S, D_ATTN), _bf16),
                   jax.ShapeDtypeStruct((B, S, LANES), _f32)],
        scratch_shapes=[pltpu.VMEM((2 * QB, D_ATTN), _bf16),
                        pltpu.VMEM((2, tq + QB, D_ATTN), _bf16),
                        pltpu.VMEM((N_HEADS, QB, 2 * QB), _f32)],
        compiler_params=_params(2),
        name=f"attn_d{dil}",
    )(flat[0], flat[1], flat[1], flat[2], flat[2])
    return num.reshape(B, C, L, D_ATTN), stats.reshape(B, C, L, LANES)


POOL_HALO = 16
N_SLAB = D_ATTN // LANES
N_BRANCH_SLAB = N_SLAB + 1


def _to_natural(o4_ref, s4_ref, o16_ref, s16_ref, nat4, nat16, cls4):
    n4, n16 = nat4.shape[1] // 4, nat4.shape[1] // 16

    def slab_of(o_ref, s_ref, c, s):
        if s < N_SLAB:
            return o_ref[c, :, s * LANES:(s + 1) * LANES].astype(_f32)
        return s_ref[c]

    for s in range(N_BRANCH_SLAB):
        for a in range(4):
            nat4.at[s][pl.ds(a, n4, stride=4), :] = slab_of(o4_ref, s4_ref, a, s)
        for r in range(16):
            cls4.at[N_BRANCH_SLAB * (r % 4) + s][pl.ds(r // 4, n16, stride=4), :] = slab_of(o16_ref, s16_ref, r, s)
    for s in range(N_BRANCH_SLAB):
        for a in range(4):
            nat16.at[s][pl.ds(a, n4, stride=4), :] = cls4[N_BRANCH_SLAB * a + s]


def _pool_chunk(ubuf, wp_ref, ps_ref, c, t0):
    ext = CHUNK + POOL_HALO
    t = t0 + jax.lax.broadcasted_iota(jnp.int32, (CHUNK, 1), 0)
    ys = []
    for g, w in enumerate(POOL_WINDOWS):
        cs = slice(g * POOL_GROUP_DIM, (g + 1) * POOL_GROUP_DIM)
        x = ubuf[c * CHUNK:c * CHUNK + ext, cs]
        win = x
        back = 1
        while back < w:
            win = win + pltpu.roll(win, back, 0)
            back *= 2
        count = jnp.minimum(t + 1, w).astype(_f32)
        tok = x[POOL_HALO:, :]
        pooled = (win[POOL_HALO:, :] / count - tok).astype(_bf16)
        y = jnp.dot(pooled, wp_ref[g], preferred_element_type=_f32) * ps_ref[:, cs]
        ys.append(y.astype(_bf16))
    return jnp.concatenate(ys, axis=1)


def _merge_chunk(o1_ref, s1_ref, nat4, nat16, c):
    rows = slice(c * CHUNK, (c + 1) * CHUNK)
    stats = (s1_ref[rows, :], nat4[N_SLAB, rows, :], nat16[N_SLAB, rows, :])
    top = jnp.maximum(jnp.maximum(stats[0], stats[1]), stats[2])
    es = [jnp.exp2(s - top) for s in stats]
    even_head = jax.lax.broadcasted_iota(jnp.int32, (CHUNK, LANES), 1) % 2 == 0
    dens = [jnp.where(even_head, pltpu.roll(s, LANES - _den_lane(0), 1), pltpu.roll(s, LANES - _den_lane(1) + 1, 1))
            for s in stats]
    inv = 1.0 / (es[0] * dens[0] + es[1] * dens[1] + es[2] * dens[2])
    wts = [e * inv for e in es]
    upper = (jax.lax.broadcasted_iota(jnp.int32, (CHUNK, LANES), 1) >= HEAD_DIM).astype(jnp.int32)
    ys = []
    for hp in range(N_HEADS // 2):
        nums = (o1_ref[rows, hp * LANES:(hp + 1) * LANES].astype(_f32), nat4[hp, rows, :], nat16[hp, rows, :])
        src_lane = 2 * hp + upper
        y = None
        for r in range(3):
            w = jnp.take_along_axis(wts[r], src_lane, axis=1, mode="promise_in_bounds")
            y = w * nums[r] if y is None else y + w * nums[r]
        ys.append(y.astype(_bf16))
    return jnp.concatenate(ys, axis=1)


def _mixffn_kernel(o1_ref, s1_ref, o4_ref, s4_ref, o16_ref, s16_ref, u_ref, uh_ref, modn_ref, wp_ref, ps_ref, wo_ref,
                   hn_ref, h_ref, mod_ref, g_ref, wg_ref, wu_ref, wd_ref, gf_ref, out_ref,
                   delta, nbuf, ubuf, nat4, nat16, cls4, *, final_norm, per_b):
    s = pl.program_id(0)
    tm = h_ref.shape[0]
    last = pl.num_programs(0) - 2
    i_next = jnp.minimum(s, last) % per_b
    gate_mix = modn_ref[5:6, :]

    def load_mixer_inputs():
        ubuf[0:POOL_HALO, :] = jnp.where(i_next == 0, 0.0, uh_ref[...])
        ubuf[POOL_HALO:, :] = u_ref[...]
        _to_natural(o4_ref, s4_ref, o16_ref, s16_ref, nat4, nat16, cls4)

    def mixed(c):
        y_pool = _pool_chunk(ubuf, wp_ref, ps_ref, c, i_next * tm + c * CHUNK)
        y_attn = _merge_chunk(o1_ref, s1_ref, nat4, nat16, c)
        return jnp.concatenate([y_pool, y_attn], axis=1)

    def project(c, y, slot):
        delta[slot, c * CHUNK:(c + 1) * CHUNK, :] = gate_mix * jnp.dot(y, wo_ref[...], preferred_element_type=_f32)

    def normalise(slot):
        x_next = hn_ref[...] + delta[slot]
        nbuf[slot] = _rms_mod(x_next, g_ref[...], modn_ref[6:7, :], modn_ref[7:8, :]).astype(_bf16)

    n_chunks = tm // CHUNK

    @pl.when(s == 0)
    def _():
        load_mixer_inputs()
        for c in range(n_chunks):
            project(c, mixed(c), 0)
        normalise(0)

    @pl.when(s > 0)
    def _():
        n = nbuf[(s - 1) % 2]
        pending = {}
        after_chunk = {0: load_mixer_inputs, 2 * n_chunks + 3: functools.partial(normalise, s % 2)}
        for c in range(n_chunks):
            after_chunk[1 + 2 * c] = functools.partial(lambda c: pending.__setitem__(c, mixed(c)), c)
            after_chunk[2 + 2 * c] = functools.partial(lambda c: project(c, pending[c], s % 2), c)
        acc = _swiglu(n, wg_ref, wu_ref, wd_ref, after_chunk)
        h = h_ref[...] + delta[(s - 1) % 2] + (0.5 * mod_ref[8:9, :]) * acc
        if final_norm:
            h = h * jax.lax.rsqrt(jnp.mean(h * h, axis=-1, keepdims=True) + EPS) * gf_ref[...]
        out_ref[...] = h


def _mix_ffn(a1, a4, a16, u, h, mod3, w_pool, pool_scale, w_out, g, wg, wu, wd, g_final, *, final_norm):
    B, S, D = h.shape
    tm = TM_MIXFFN
    per_b = S // tm
    n_tiles = B * per_b
    assert tm % CHUNK == 0

    def tile(t):
        return t // per_b, t % per_b

    def nxt(s):
        return tile(jnp.minimum(s, n_tiles - 1))

    def cur(s):
        return tile(jnp.maximum(s - 1, 0))

    def branch_spec(dil, width):
        if dil == 1:
            return pl.BlockSpec((None, None, tm, width), lambda s: (nxt(s)[0], 0, nxt(s)[1], 0))
        return pl.BlockSpec((None, dil, tm // dil, width), lambda s: (nxt(s)[0], 0, nxt(s)[1], 0))

    row_tile = pl.BlockSpec((None, tm, D), lambda s: (cur(s)[0], cur(s)[1], 0))
    return pl.pallas_call(
        functools.partial(_mixffn_kernel, final_norm=final_norm, per_b=per_b),
        grid=(n_tiles + 1,),
        in_specs=[branch_spec(1, D_ATTN), branch_spec(1, LANES),
                  branch_spec(4, D_ATTN), branch_spec(4, LANES),
                  branch_spec(16, D_ATTN), branch_spec(16, LANES),
                  pl.BlockSpec((None, tm, D_POOL), lambda s: (nxt(s)[0], nxt(s)[1], 0)),
                  pl.BlockSpec((None, POOL_HALO, D_POOL),
                               lambda s: (nxt(s)[0], jnp.maximum(nxt(s)[1] * (tm // POOL_HALO) - 1, 0), 0)),
                  pl.BlockSpec((None, N_MOD, D), lambda s: (nxt(s)[0], 0, 0)),
                  _resident(w_pool.shape),
                  _resident((1, D_POOL)),
                  _resident(w_out.shape),
                  pl.BlockSpec((None, tm, D), lambda s: (nxt(s)[0], nxt(s)[1], 0)),
                  row_tile,
                  pl.BlockSpec((None, N_MOD, D), lambda s: (cur(s)[0], 0, 0)),
                  _resident((1, D)),
                  _resident(wg.shape),
                  _resident(wu.shape),
                  _resident(wd.shape),
                  _resident((1, D))],
        out_specs=row_tile,
        out_shape=jax.ShapeDtypeStruct((B, S, D), _f32),
        scratch_shapes=[pltpu.VMEM((2, tm, D), _f32),
                        pltpu.VMEM((2, tm, D), _bf16),
                        pltpu.VMEM((tm + POOL_HALO, D_POOL), _f32),
                        pltpu.VMEM((N_BRANCH_SLAB, tm, LANES), _f32),
                        pltpu.VMEM((N_BRANCH_SLAB, tm, LANES), _f32),
                        pltpu.VMEM((N_BRANCH_SLAB * 4, tm // 4, LANES), _f32)],
        compiler_params=_params(1),
        name="mix_ffn",
    )(*a1, *a4, *a16, u, u, mod3, w_pool, pool_scale.reshape(1, D_POOL), w_out,
      h, h, mod3, g.reshape(1, D), wg, wu, wd, g_final.reshape(1, D))


def kernel(x, c, w_ada, b_ada, g_ffn1, w1_gate, w1_up, w1_down, g_mix, w_in, w_pool, pool_scale, w_out,
           g_ffn2, w2_gate, w2_up, w2_down, g_final):
    B, S, D = x.shape
    depth = w_ada.shape[0]
    p4 = jnp.asarray(_sort_matrix(4), _bf16)
    p16 = jnp.asarray(_sort_matrix(16), _bf16)
    h = x
    for l in range(depth):
        mod, (wg1, wu1, wd1) = _ada_mod(c, w_ada[l], b_ada[l], (w1_gate[l], w1_up[l], w1_down[l]))
        mod3 = mod.reshape(B, N_MOD, D)
        pool_flat = w_pool[l].reshape(N_POOL_GROUPS * POOL_GROUP_DIM, POOL_GROUP_DIM)
        h, (wg2, wu2, wd2, w_in_b, w_out_b, w_pool_b) = _ffn(
            h, mod3, g_ffn1[l], wg1, wu1, wd1,
            narrow=(w2_gate[l], w2_up[l], w2_down[l], w_in[l], w_out[l], pool_flat))
        u, q1, k1, v1, q4, k4, v4, q16, k16, v16 = _in_proj(h, mod3, g_mix[l], w_in_b, p4, p16)
        a1 = _attention(q1[:, None], k1[:, None], v1[:, None], 1)
        a4 = _attention(q4, k4, v4, 4, tq=ATTN_ROWS // 2)
        a16 = _attention(q16, k16, v16, 16)
        h = _mix_ffn(a1, a4, a16, u, h, mod3, w_pool_b.reshape(w_pool[l].shape), pool_scale[l], w_out_b,
                     g_ffn2[l], wg2, wu2, wd2, g_final, final_norm=(l == depth - 1))
    return h
```

```python
import functools
import math

import jax
import jax.numpy as jnp
import numpy as np
from jax.experimental import pallas as pl
from jax.experimental.pallas import tpu as pltpu

D_MODEL = 1024
D_POOL = 512
D_ATTN = 512
POOL_WINDOWS = (2, 4, 8, 16)
POOL_GROUP_DIM = 128
N_POOL_GROUPS = len(POOL_WINDOWS)
HEAD_DIM = 64
N_HEADS = 8
WINDOW = 128
D_FF = 2816
N_MOD = 9
EPS = 1e-6
NEG = -1e30
LOG2E = math.log2(math.e)

LANES = 128
PERM_ROWS = 256
CHUNK = 256
VMEM_LIMIT = 56 * 1024 * 1024

TM_FFN = 1024
TM_PROJ = 1024
TM_MIXFFN = 512
ATTN_ROWS = 2048
BN_ADA = 1152

_f32 = jnp.float32
_bf16 = jnp.bfloat16


def _rms_mod(x, g, shift, scale):
    r = jax.lax.rsqrt(jnp.mean(x * x, axis=-1, keepdims=True) + EPS)
    return x * r * (g * (1.0 + scale)) + shift


def _resident(shape):
    nd = len(shape)
    return pl.BlockSpec(shape, lambda *_: (0,) * nd, pipeline_mode=pl.Buffered(1))


def _params(n_axes):
    return pltpu.CompilerParams(dimension_semantics=("arbitrary",) * n_axes, vmem_limit_bytes=VMEM_LIMIT)


def _ada_kernel(c_ref, w_ref, b_ref, *refs):
    n_cast = len(refs) // 2
    o_ref = refs[n_cast]
    c = c_ref[...]
    a = (c * (1.0 / (1.0 + jnp.exp(-c)))).astype(_bf16)
    o_ref[...] = jnp.dot(a, w_ref[...].astype(_bf16), preferred_element_type=_f32) + b_ref[...]
    for src, dst in zip(refs[:n_cast], refs[n_cast + 1:]):
        dst[...] = src[...].astype(_bf16)


def _ada_mod(c, w_ada, b_ada, narrow):
    B, D = c.shape
    N = w_ada.shape[1]
    steps = N // BN_ADA

    def cast_spec(w):
        return pl.BlockSpec((w.shape[0] // steps, w.shape[1]), lambda j: (j, 0))

    outs = pl.pallas_call(
        _ada_kernel,
        grid=(steps,),
        in_specs=[pl.BlockSpec((B, D), lambda j: (0, 0)),
                  pl.BlockSpec((D, BN_ADA), lambda j: (0, j)),
                  pl.BlockSpec((1, BN_ADA), lambda j: (0, j))] + [cast_spec(w) for w in narrow],
        out_specs=[pl.BlockSpec((B, BN_ADA), lambda j: (0, j))] + [cast_spec(w) for w in narrow],
        out_shape=[jax.ShapeDtypeStruct((B, N), _f32)] + [jax.ShapeDtypeStruct(w.shape, _bf16) for w in narrow],
        compiler_params=_params(1),
        name="ada_mod",
    )(c, w_ada, b_ada.reshape(1, N), *narrow)
    return outs[0], outs[1:]


FF_CHUNK = 256
N_CAST_BLOCKS = 16


def _swiglu(n, wg_ref, wu_ref, wd_ref, after_chunk=None):
    acc = jnp.zeros((n.shape[0], wd_ref.shape[1]), _f32)
    for c in range(D_FF // FF_CHUNK):
        cs = slice(c * FF_CHUNK, (c + 1) * FF_CHUNK)
        g = jnp.dot(n, wg_ref[:, cs], preferred_element_type=_f32)
        u = jnp.dot(n, wu_ref[:, cs], preferred_element_type=_f32)
        a = (g * (1.0 / (1.0 + jnp.exp(-g))) * u).astype(_bf16)
        acc = acc + jnp.dot(a, wd_ref[cs, :], preferred_element_type=_f32)
        if after_chunk and c in after_chunk:
            after_chunk[c]()
    return acc


def _ffn_kernel(x_ref, mod_ref, g_ref, wg_ref, wu_ref, wd_ref, *refs):
    n_cast = len(refs) // 2
    o_ref = refs[n_cast]
    x = x_ref[...]
    n = _rms_mod(x, g_ref[...], mod_ref[0:1, :], mod_ref[1:2, :]).astype(_bf16)
    for src, dst in zip(refs[:n_cast], refs[n_cast + 1:]):
        dst[...] = src[...].astype(_bf16)
    o_ref[...] = x + (0.5 * mod_ref[2:3, :]) * _swiglu(n, wg_ref, wu_ref, wd_ref)


def _ffn(x, mod3, g, wg, wu, wd, narrow):
    B, S, D = x.shape
    tm = TM_FFN
    steps = B * (S // tm)
    per_b = S // tm

    def cast_spec(w):
        every = steps // N_CAST_BLOCKS
        return pl.BlockSpec((w.shape[0] // N_CAST_BLOCKS, w.shape[1]),
                            lambda b, i: ((b * per_b + i) // every, 0))

    row_tile = pl.BlockSpec((None, tm, D), lambda b, i: (b, i, 0))
    outs = pl.pallas_call(
        _ffn_kernel,
        grid=(B, per_b),
        in_specs=[row_tile,
                  pl.BlockSpec((None, N_MOD, D), lambda b, i: (b, 0, 0)),
                  _resident((1, D)),
                  _resident(wg.shape),
                  _resident(wu.shape),
                  _resident(wd.shape)] + [cast_spec(w) for w in narrow],
        out_specs=[row_tile] + [cast_spec(w) for w in narrow],
        out_shape=[jax.ShapeDtypeStruct((B, S, D), _f32)]
                  + [jax.ShapeDtypeStruct(w.shape, _bf16) for w in narrow],
        compiler_params=_params(2),
        name="ffn",
    )(x, mod3, g.reshape(1, D), wg, wu, wd, *narrow)
    return outs[0], outs[1:]


def _sort_matrix(dil):
    per = PERM_ROWS // dil
    i = np.arange(PERM_ROWS)
    src = (i % per) * dil + i // per
    p = np.zeros((PERM_ROWS, PERM_ROWS), np.float32)
    p[i, src] = 1.0
    return p


POOL_HALO = 16


def _pool_windows(ubuf, c, t0):
    ext = CHUNK + POOL_HALO
    t = t0 + jax.lax.broadcasted_iota(jnp.int32, (CHUNK, 1), 0)
    groups = []
    for g, w in enumerate(POOL_WINDOWS):
        cs = slice(g * POOL_GROUP_DIM, (g + 1) * POOL_GROUP_DIM)
        x = ubuf[c * CHUNK:c * CHUNK + ext, cs]
        win = x
        back = 1
        while back < w:
            win = win + pltpu.roll(win, back, 0)
            back *= 2
        count = jnp.minimum(t + 1, w).astype(_f32)
        groups.append((win[POOL_HALO:, :] / count - x[POOL_HALO:, :]).astype(_bf16))
    return jnp.concatenate(groups, axis=1)


def _inproj_kernel(h_ref, mod_ref, g_ref, w_ref, p4_ref, p16_ref,
                   pooled_ref, q1_ref, k1_ref, v1_ref, q4_ref, k4_ref, v4_ref, q16_ref, k16_ref, v16_ref,
                   ubuf, utail):
    i = pl.program_id(1)
    x = h_ref[...]
    tm = x.shape[0]
    n = _rms_mod(x, g_ref[...], mod_ref[3:4, :], mod_ref[4:5, :]).astype(_bf16)
    u = jnp.dot(n, w_ref[:, :D_POOL], preferred_element_type=_f32)
    z = jnp.dot(n, w_ref[:, D_POOL:], preferred_element_type=_f32)
    ubuf[0:POOL_HALO, :] = jnp.where(i == 0, 0.0, utail[...])
    ubuf[POOL_HALO:, :] = u
    utail[...] = u[tm - POOL_HALO:, :]
    q = (z[:, :D_ATTN] * (HEAD_DIM ** -0.5 * LOG2E)).astype(_bf16)
    k = z[:, D_ATTN:2 * D_ATTN].astype(_bf16)
    v = z[:, 2 * D_ATTN:].astype(_bf16)
    q1_ref[...] = q
    k1_ref[...] = k
    v1_ref[...] = v
    qkv = jnp.concatenate([q, k, v], axis=1)
    for dil, p_ref, outs in ((4, p4_ref, (q4_ref, k4_ref, v4_ref)),
                             (16, p16_ref, (q16_ref, k16_ref, v16_ref))):
        per = PERM_ROWS // dil
        for c in range(tm // PERM_ROWS):
            s = jnp.dot(p_ref[...], qkv[c * PERM_ROWS:(c + 1) * PERM_ROWS, :],
                        preferred_element_type=_f32).astype(_bf16)
            if dil == 4:
                pooled_ref[c * CHUNK:(c + 1) * CHUNK, :] = _pool_windows(ubuf, c, i * tm + c * CHUNK)
            for a, o_ref in enumerate(outs):
                for cls in range(dil):
                    o_ref[cls, c * per:(c + 1) * per, :] = s[cls * per:(cls + 1) * per,
                                                             a * D_ATTN:(a + 1) * D_ATTN]


def _in_proj(h, mod3, g, w_in, p4, p16):
    B, S, D = h.shape
    tm = TM_PROJ
    nat = jax.ShapeDtypeStruct((B, S, D_ATTN), _bf16)
    nat_spec = pl.BlockSpec((None, tm, D_ATTN), lambda b, i: (b, i, 0))

    def cls_shape(dil):
        return jax.ShapeDtypeStruct((B, dil, S // dil, D_ATTN), _bf16)

    def cls_spec(dil):
        return pl.BlockSpec((None, dil, tm // dil, D_ATTN), lambda b, i: (b, 0, i, 0))

    return pl.pallas_call(
        _inproj_kernel,
        grid=(B, S // tm),
        in_specs=[pl.BlockSpec((None, tm, D), lambda b, i: (b, i, 0)),
                  pl.BlockSpec((None, N_MOD, D), lambda b, i: (b, 0, 0)),
                  _resident((1, D)),
                  _resident(w_in.shape),
                  _resident(p4.shape),
                  _resident(p16.shape)],
        out_specs=[nat_spec, nat_spec, nat_spec, nat_spec,
                   cls_spec(4), cls_spec(4), cls_spec(4),
                   cls_spec(16), cls_spec(16), cls_spec(16)],
        out_shape=[nat, nat, nat, nat,
                   cls_shape(4), cls_shape(4), cls_shape(4),
                   cls_shape(16), cls_shape(16), cls_shape(16)],
        scratch_shapes=[pltpu.VMEM((tm + POOL_HALO, D_POOL), _f32), pltpu.VMEM((POOL_HALO, D_POOL), _f32)],
        compiler_params=_params(2),
        name="in_proj",
    )(h, mod3, g.reshape(1, D), w_in, p4, p16)


QB = 128


def _den_lane(h):
    return N_HEADS + h + (HEAD_DIM if h % 2 == 0 else 0)


def _attn_kernel(q_ref, km_ref, kh_ref, vm_ref, vh_ref, o_ref, st_ref, kbuf, vbuf, bias, *, dil, seq_len):
    b, i = pl.program_id(0), pl.program_id(1)
    tq = q_ref.shape[0]

    @pl.when((b == 0) & (i == 0))
    def _():
        row = jax.lax.broadcasted_iota(jnp.int32, (QB, 2 * QB), 0)
        col = jax.lax.broadcasted_iota(jnp.int32, (QB, 2 * QB), 1)
        delta = row - col + QB
        valid = (delta >= 0) & (delta <= WINDOW)
        dist = (delta * dil).astype(_f32)
        for h in range(N_HEADS):
            slope = 2.0 ** (-8.0 * (h + 1) / N_HEADS) * LOG2E
            bias[h] = jnp.where(valid, -slope * dist, NEG)

    kbuf[0:QB, :] = kh_ref[...]
    kbuf[QB:, :] = km_ref[0:QB, :]
    own_even = jax.lax.broadcasted_iota(jnp.int32, (1, D_ATTN), 1) % LANES < HEAD_DIM
    for dst, src in ((slice(0, QB), vh_ref), (slice(QB, None), vm_ref)):
        v = src[...]
        ones = jnp.ones_like(v)
        vbuf[0, dst, :] = jnp.where(own_even, v, ones)
        vbuf[1, dst, :] = jnp.where(own_even, ones, v)

    lane = jax.lax.broadcasted_iota(jnp.int32, (QB, LANES), 1)
    low_half = lane < HEAD_DIM
    prev_half = jax.lax.broadcasted_iota(jnp.int32, (1, 2 * QB), 1) < QB
    if seq_len >= tq:
        seq_start = {0: jnp.where(prev_half & (i % (seq_len // tq) == 0), NEG, 0.0)}
    else:
        seq_start = {j: jnp.where(prev_half, NEG, 0.0) for j in range(0, tq // QB, seq_len // QB)}

    for j in range(tq // QB):
        rows = slice(j * QB, (j + 1) * QB)
        stats = jnp.zeros((QB, LANES), _f32)
        for hp in range(N_HEADS // 2):
            cs = slice(hp * LANES, (hp + 1) * LANES)
            q2 = q_ref[rows, cs]
            k2 = kbuf[:, cs] if j == 0 else km_ref[(j - 1) * QB:(j + 1) * QB, cs]
            halves = []
            for e in range(2):
                h = 2 * hp + e
                qm = jnp.where(low_half if e == 0 else ~low_half, q2, jnp.zeros_like(q2))
                s = jax.lax.dot_general(qm, k2, (((1,), (1,)), ((), ())), preferred_element_type=_f32)
                s = s + bias[h]
                if j in seq_start:
                    s = s + seq_start[j]
                m = jnp.max(s, axis=-1, keepdims=True)
                p = jnp.exp2(s - m).astype(_bf16)
                pv = jnp.dot(p, vbuf[e, j * QB:(j + 2) * QB, cs], preferred_element_type=_f32)
                halves.append(pv)
                stats = jnp.where(lane == h, m, stats)
                stats = jnp.where(lane == _den_lane(h), pv, stats)
            o_ref[rows, cs] = jnp.where(low_half, halves[0], halves[1]).astype(_bf16)
        st_ref[rows, :] = stats


def _attention(q, k, v, dil, tq=ATTN_ROWS):
    B, C, L, _ = q.shape
    S = C * L
    flat = [a.reshape(B, S, D_ATTN) for a in (q, k, v)]
    main = pl.BlockSpec((None, tq, D_ATTN), lambda b, i: (b, i, 0))
    halo = pl.BlockSpec((None, QB, D_ATTN), lambda b, i: (b, jnp.maximum(i * (tq // QB) - 1, 0), 0))
    num, stats = pl.pallas_call(
        functools.partial(_attn_kernel, dil=dil, seq_len=L),
        grid=(B, S // tq),
        in_specs=[main, main, halo, main, halo],
        out_specs=[main, pl.BlockSpec((None, tq, LANES), lambda b, i: (b, i, 0))],
        out_shape=[jax.ShapeDtypeStruct((B, S, D_ATTN), _bf16),
                   jax.ShapeDtypeStruct((B, S, LANES), _f32)],
        scratch_shapes=[pltpu.VMEM((2 * QB, D_ATTN), _bf16),
                        pltpu.VMEM((2, tq + QB, D_ATTN), _bf16),
                        pltpu.VMEM((N_HEADS, QB, 2 * QB), _f32)],
        compiler_params=_params(2),
        name=f"attn_d{dil}",
    )(flat[0], flat[1], flat[1], flat[2], flat[2])
    return num.reshape(B, C, L, D_ATTN), stats.reshape(B, C, L, LANES)


N_SLAB = D_ATTN // LANES
N_BRANCH_SLAB = N_SLAB + 1


def _to_natural(o4_ref, s4_ref, o16_ref, s16_ref, nat4, nat16, cls4):
    n4, n16 = nat4.shape[1] // 4, nat4.shape[1] // 16

    def slab_of(o_ref, s_ref, c, s):
        if s < N_SLAB:
            return o_ref[c, :, s * LANES:(s + 1) * LANES].astype(_f32)
        return s_ref[c]

    for s in range(N_BRANCH_SLAB):
        for a in range(4):
            nat4.at[s][pl.ds(a, n4, stride=4), :] = slab_of(o4_ref, s4_ref, a, s)
        for r in range(16):
            cls4.at[N_BRANCH_SLAB * (r % 4) + s][pl.ds(r // 4, n16, stride=4), :] = slab_of(o16_ref, s16_ref, r, s)
    for s in range(N_BRANCH_SLAB):
        for a in range(4):
            nat16.at[s][pl.ds(a, n4, stride=4), :] = cls4[N_BRANCH_SLAB * a + s]


def _pool_chunk(pooled_ref, wp_ref, ps_ref, c):
    rows = slice(c * CHUNK, (c + 1) * CHUNK)
    ys = []
    for g in range(N_POOL_GROUPS):
        cs = slice(g * POOL_GROUP_DIM, (g + 1) * POOL_GROUP_DIM)
        y = jnp.dot(pooled_ref[rows, cs], wp_ref[g], preferred_element_type=_f32) * ps_ref[:, cs]
        ys.append(y.astype(_bf16))
    return jnp.concatenate(ys, axis=1)


def _merge_chunk(o1_ref, s1_ref, nat4, nat16, c):
    rows = slice(c * CHUNK, (c + 1) * CHUNK)
    stats = (s1_ref[rows, :], nat4[N_SLAB, rows, :], nat16[N_SLAB, rows, :])
    top = jnp.maximum(jnp.maximum(stats[0], stats[1]), stats[2])
    es = [jnp.exp2(s - top) for s in stats]
    even_head = jax.lax.broadcasted_iota(jnp.int32, (CHUNK, LANES), 1) % 2 == 0
    dens = [jnp.where(even_head, pltpu.roll(s, LANES - _den_lane(0), 1), pltpu.roll(s, LANES - _den_lane(1) + 1, 1))
            for s in stats]
    inv = 1.0 / (es[0] * dens[0] + es[1] * dens[1] + es[2] * dens[2])
    wts = [e * inv for e in es]
    upper = (jax.lax.broadcasted_iota(jnp.int32, (CHUNK, LANES), 1) >= HEAD_DIM).astype(jnp.int32)
    ys = []
    for hp in range(N_HEADS // 2):
        nums = (o1_ref[rows, hp * LANES:(hp + 1) * LANES].astype(_f32), nat4[hp, rows, :], nat16[hp, rows, :])
        src_lane = 2 * hp + upper
        y = None
        for r in range(3):
            w = jnp.take_along_axis(wts[r], src_lane, axis=1, mode="promise_in_bounds")
            y = w * nums[r] if y is None else y + w * nums[r]
        ys.append(y.astype(_bf16))
    return jnp.concatenate(ys, axis=1)


def _mixffn_kernel(o1_ref, s1_ref, o4_ref, s4_ref, o16_ref, s16_ref, pooled_ref, modn_ref, wp_ref, ps_ref, wo_ref,
                   hn_ref, h_ref, mod_ref, g_ref, wg_ref, wu_ref, wd_ref, gf_ref, out_ref,
                   delta, nbuf, nat4, nat16, cls4, *, final_norm):
    s = pl.program_id(0)
    tm = h_ref.shape[0]
    gate_mix = modn_ref[5:6, :]

    def load_mixer_inputs():
        _to_natural(o4_ref, s4_ref, o16_ref, s16_ref, nat4, nat16, cls4)

    def mixed(c):
        y_pool = _pool_chunk(pooled_ref, wp_ref, ps_ref, c)
        y_attn = _merge_chunk(o1_ref, s1_ref, nat4, nat16, c)
        return jnp.concatenate([y_pool, y_attn], axis=1)

    def project(c, y, slot):
        delta[slot, c * CHUNK:(c + 1) * CHUNK, :] = gate_mix * jnp.dot(y, wo_ref[...], preferred_element_type=_f32)

    def normalise(slot):
        x_next = hn_ref[...] + delta[slot]
        nbuf[slot] = _rms_mod(x_next, g_ref[...], modn_ref[6:7, :], modn_ref[7:8, :]).astype(_bf16)

    n_chunks = tm // CHUNK

    @pl.when(s == 0)
    def _():
        load_mixer_inputs()
        for c in range(n_chunks):
            project(c, mixed(c), 0)
        normalise(0)

    @pl.when(s > 0)
    def _():
        n = nbuf[(s - 1) % 2]
        pending = {}
        after_chunk = {0: load_mixer_inputs, 2 * n_chunks + 3: functools.partial(normalise, s % 2)}
        for c in range(n_chunks):
            after_chunk[1 + 2 * c] = functools.partial(lambda c: pending.__setitem__(c, mixed(c)), c)
            after_chunk[2 + 2 * c] = functools.partial(lambda c: project(c, pending[c], s % 2), c)
        acc = _swiglu(n, wg_ref, wu_ref, wd_ref, after_chunk)
        h = h_ref[...] + delta[(s - 1) % 2] + (0.5 * mod_ref[8:9, :]) * acc
        if final_norm:
            h = h * jax.lax.rsqrt(jnp.mean(h * h, axis=-1, keepdims=True) + EPS) * gf_ref[...]
        out_ref[...] = h


def _mix_ffn(a1, a4, a16, pooled, h, mod3, w_pool, pool_scale, w_out, g, wg, wu, wd, g_final, *, final_norm):
    B, S, D = h.shape
    tm = TM_MIXFFN
    per_b = S // tm
    n_tiles = B * per_b
    assert tm % CHUNK == 0

    def tile(t):
        return t // per_b, t % per_b

    def nxt(s):
        return tile(jnp.minimum(s, n_tiles - 1))

    def cur(s):
        return tile(jnp.maximum(s - 1, 0))

    def branch_spec(dil, width):
        if dil == 1:
            return pl.BlockSpec((None, None, tm, width), lambda s: (nxt(s)[0], 0, nxt(s)[1], 0))
        return pl.BlockSpec((None, dil, tm // dil, width), lambda s: (nxt(s)[0], 0, nxt(s)[1], 0))

    row_tile = pl.BlockSpec((None, tm, D), lambda s: (cur(s)[0], cur(s)[1], 0))
    return pl.pallas_call(
        functools.partial(_mixffn_kernel, final_norm=final_norm),
        grid=(n_tiles + 1,),
        in_specs=[branch_spec(1, D_ATTN), branch_spec(1, LANES),
                  branch_spec(4, D_ATTN), branch_spec(4, LANES),
                  branch_spec(16, D_ATTN), branch_spec(16, LANES),
                  pl.BlockSpec((None, tm, D_POOL), lambda s: (nxt(s)[0], nxt(s)[1], 0)),
                  pl.BlockSpec((None, N_MOD, D), lambda s: (nxt(s)[0], 0, 0)),
                  _resident(w_pool.shape),
                  _resident((1, D_POOL)),
                  _resident(w_out.shape),
                  pl.BlockSpec((None, tm, D), lambda s: (nxt(s)[0], nxt(s)[1], 0)),
                  row_tile,
                  pl.BlockSpec((None, N_MOD, D), lambda s: (cur(s)[0], 0, 0)),
                  _resident((1, D)),
                  _resident(wg.shape),
                  _resident(wu.shape),
                  _resident(wd.shape),
                  _resident((1, D))],
        out_specs=row_tile,
        out_shape=jax.ShapeDtypeStruct((B, S, D), _f32),
        scratch_shapes=[pltpu.VMEM((2, tm, D), _f32),
                        pltpu.VMEM((2, tm, D), _bf16),
                        pltpu.VMEM((N_BRANCH_SLAB, tm, LANES), _f32),
                        pltpu.VMEM((N_BRANCH_SLAB, tm, LANES), _f32),
                        pltpu.VMEM((N_BRANCH_SLAB * 4, tm // 4, LANES), _f32)],
        compiler_params=_params(1),
        name="mix_ffn",
    )(*a1, *a4, *a16, pooled, mod3, w_pool, pool_scale.reshape(1, D_POOL), w_out,
      h, h, mod3, g.reshape(1, D), wg, wu, wd, g_final.reshape(1, D))


def kernel(x, c, w_ada, b_ada, g_ffn1, w1_gate, w1_up, w1_down, g_mix, w_in, w_pool, pool_scale, w_out,
           g_ffn2, w2_gate, w2_up, w2_down, g_final):
    B, S, D = x.shape
    depth = w_ada.shape[0]
    p4 = jnp.asarray(_sort_matrix(4), _bf16)
    p16 = jnp.asarray(_sort_matrix(16), _bf16)
    h = x
    for l in range(depth):
        mod, (wg1, wu1, wd1) = _ada_mod(c, w_ada[l], b_ada[l], (w1_gate[l], w1_up[l], w1_down[l]))
        mod3 = mod.reshape(B, N_MOD, D)
        pool_flat = w_pool[l].reshape(N_POOL_GROUPS * POOL_GROUP_DIM, POOL_GROUP_DIM)
        h, (wg2, wu2, wd2, w_in_b, w_out_b, w_pool_b) = _ffn(
            h, mod3, g_ffn1[l], wg1, wu1, wd1,
            narrow=(w2_gate[l], w2_up[l], w2_down[l], w_in[l], w_out[l], pool_flat))
        pooled, q1, k1, v1, q4, k4, v4, q16, k16, v16 = _in_proj(h, mod3, g_mix[l], w_in_b, p4, p16)
        a1 = _attention(q1[:, None], k1[:, None], v1[:, None], 1)
        a4 = _attention(q4, k4, v4, 4, tq=ATTN_ROWS // 2)
        a16 = _attention(q16, k16, v16, 16)
        h = _mix_ffn(a1, a4, a16, pooled, h, mod3, w_pool_b.reshape(w_pool[l].shape), pool_scale[l], w_out_b,
                     g_ffn2[l], wg2, wu2, wd2, g_final, final_norm=(l == depth - 1))
    return h
```

```python
import functools
import math

import jax
import jax.numpy as jnp
from jax.experimental import pallas as pl
from jax.experimental.pallas import tpu as pltpu

D_MODEL = 1024
D_POOL = 512
D_ATTN = 512
POOL_WINDOWS = (2, 4, 8, 16)
POOL_GROUP_DIM = 128
N_POOL_GROUPS = len(POOL_WINDOWS)
HEAD_DIM = 64
N_HEADS = 8
WINDOW = 128
D_FF = 2816
N_MOD = 9
EPS = 1e-6
NEG = -1e30
LOG2E = math.log2(math.e)

LANES = 128
CHUNK = 256
VMEM_LIMIT = 56 * 1024 * 1024

TM_FFN = 1024
TM_PROJ = 1024
TM_MIXFFN = 512
ATTN_ROWS = 2048
BN_ADA = 1152

_f32 = jnp.float32
_bf16 = jnp.bfloat16


def _rms_mod(x, g, shift, scale):
    r = jax.lax.rsqrt(jnp.mean(x * x, axis=-1, keepdims=True) + EPS)
    return x * r * (g * (1.0 + scale)) + shift


def _resident(shape):
    nd = len(shape)
    return pl.BlockSpec(shape, lambda *_: (0,) * nd, pipeline_mode=pl.Buffered(1))


def _params(n_axes):
    return pltpu.CompilerParams(dimension_semantics=("arbitrary",) * n_axes, vmem_limit_bytes=VMEM_LIMIT)


def _ada_kernel(c_ref, w_ref, b_ref, *refs):
    n_cast = len(refs) // 2
    o_ref = refs[n_cast]
    c = c_ref[...]
    a = (c * (1.0 / (1.0 + jnp.exp(-c)))).astype(_bf16)
    o_ref[...] = jnp.dot(a, w_ref[...].astype(_bf16), preferred_element_type=_f32) + b_ref[...]
    for src, dst in zip(refs[:n_cast], refs[n_cast + 1:]):
        dst[...] = src[...].astype(_bf16)


def _ada_mod(c, w_ada, b_ada, narrow):
    B, D = c.shape
    N = w_ada.shape[1]
    steps = N // BN_ADA

    def cast_spec(w):
        return pl.BlockSpec((w.shape[0] // steps, w.shape[1]), lambda j: (j, 0))

    outs = pl.pallas_call(
        _ada_kernel,
        grid=(steps,),
        in_specs=[pl.BlockSpec((B, D), lambda j: (0, 0)),
                  pl.BlockSpec((D, BN_ADA), lambda j: (0, j)),
                  pl.BlockSpec((1, BN_ADA), lambda j: (0, j))] + [cast_spec(w) for w in narrow],
        out_specs=[pl.BlockSpec((B, BN_ADA), lambda j: (0, j))] + [cast_spec(w) for w in narrow],
        out_shape=[jax.ShapeDtypeStruct((B, N), _f32)] + [jax.ShapeDtypeStruct(w.shape, _bf16) for w in narrow],
        compiler_params=_params(1),
        name="ada_mod",
    )(c, w_ada, b_ada.reshape(1, N), *narrow)
    return outs[0], outs[1:]


FF_CHUNK = 256
N_CAST_BLOCKS = 16


def _swiglu(n, wg_ref, wu_ref, wd_ref, after_chunk=None):
    acc = jnp.zeros((n.shape[0], wd_ref.shape[1]), _f32)
    for c in range(D_FF // FF_CHUNK):
        cs = slice(c * FF_CHUNK, (c + 1) * FF_CHUNK)
        g = jnp.dot(n, wg_ref[:, cs], preferred_element_type=_f32)
        u = jnp.dot(n, wu_ref[:, cs], preferred_element_type=_f32)
        a = (g * (1.0 / (1.0 + jnp.exp(-g))) * u).astype(_bf16)
        acc = acc + jnp.dot(a, wd_ref[cs, :], preferred_element_type=_f32)
        if after_chunk and c in after_chunk:
            after_chunk[c]()
    return acc


def _ffn_kernel(x_ref, mod_ref, g_ref, wg_ref, wu_ref, wd_ref, *refs):
    n_cast = len(refs) // 2
    o_ref = refs[n_cast]
    x = x_ref[...]
    n = _rms_mod(x, g_ref[...], mod_ref[0:1, :], mod_ref[1:2, :]).astype(_bf16)
    for src, dst in zip(refs[:n_cast], refs[n_cast + 1:]):
        dst[...] = src[...].astype(_bf16)
    o_ref[...] = x + (0.5 * mod_ref[2:3, :]) * _swiglu(n, wg_ref, wu_ref, wd_ref)


def _ffn(x, mod3, g, wg, wu, wd, narrow):
    B, S, D = x.shape
    tm = TM_FFN
    steps = B * (S // tm)
    per_b = S // tm

    def cast_spec(w):
        every = steps // N_CAST_BLOCKS
        return pl.BlockSpec((w.shape[0] // N_CAST_BLOCKS, w.shape[1]),
                            lambda b, i: ((b * per_b + i) // every, 0))

    row_tile = pl.BlockSpec((None, tm, D), lambda b, i: (b, i, 0))
    outs = pl.pallas_call(
        _ffn_kernel,
        grid=(B, per_b),
        in_specs=[row_tile,
                  pl.BlockSpec((None, N_MOD, D), lambda b, i: (b, 0, 0)),
                  _resident((1, D)),
                  _resident(wg.shape),
                  _resident(wu.shape),
                  _resident(wd.shape)] + [cast_spec(w) for w in narrow],
        out_specs=[row_tile] + [cast_spec(w) for w in narrow],
        out_shape=[jax.ShapeDtypeStruct((B, S, D), _f32)]
                  + [jax.ShapeDtypeStruct(w.shape, _bf16) for w in narrow],
        compiler_params=_params(2),
        name="ffn",
    )(x, mod3, g.reshape(1, D), wg, wu, wd, *narrow)
    return outs[0], outs[1:]


POOL_HALO = 16
N_SLAB = D_ATTN // LANES


def _pool_windows(ubuf, c, t0):
    ext = CHUNK + POOL_HALO
    t = t0 + jax.lax.broadcasted_iota(jnp.int32, (CHUNK, 1), 0)
    groups = []
    for g, w in enumerate(POOL_WINDOWS):
        cs = slice(g * POOL_GROUP_DIM, (g + 1) * POOL_GROUP_DIM)
        x = ubuf[c * CHUNK:c * CHUNK + ext, cs]
        win = x
        back = 1
        while back < w:
            win = win + pltpu.roll(win, back, 0)
            back *= 2
        count = jnp.minimum(t + 1, w).astype(_f32)
        groups.append((win[POOL_HALO:, :] / count - x[POOL_HALO:, :]).astype(_bf16))
    return jnp.concatenate(groups, axis=1)


def _inproj_kernel(h_ref, mod_ref, g_ref, w_ref,
                   pooled_ref, q1_ref, k1_ref, v1_ref, q4_ref, k4_ref, v4_ref, q16_ref, k16_ref, v16_ref,
                   ubuf, utail, nat, cls4):
    i = pl.program_id(1)
    x = h_ref[...]
    tm = x.shape[0]
    n4, n16 = tm // 4, tm // 16
    n = _rms_mod(x, g_ref[...], mod_ref[3:4, :], mod_ref[4:5, :]).astype(_bf16)
    z = jnp.dot(n, w_ref[...], preferred_element_type=_f32)
    ubuf[0:POOL_HALO, :] = jnp.where(i == 0, 0.0, utail[...])
    ubuf[POOL_HALO:, :] = z[:, :D_POOL]
    utail[...] = z[tm - POOL_HALO:, :D_POOL]
    for c in range(tm // CHUNK):
        pooled_ref[c * CHUNK:(c + 1) * CHUNK, :] = _pool_windows(ubuf, c, i * tm + c * CHUNK)
    q = z[:, D_POOL:D_POOL + D_ATTN] * (HEAD_DIM ** -0.5 * LOG2E)
    k = z[:, D_POOL + D_ATTN:D_POOL + 2 * D_ATTN]
    v = z[:, D_POOL + 2 * D_ATTN:]
    for t, (arr, o1, o4, o16) in enumerate(((q, q1_ref, q4_ref, q16_ref), (k, k1_ref, k4_ref, k16_ref),
                                            (v, v1_ref, v4_ref, v16_ref))):
        o1[...] = arr.astype(_bf16)
        for s in range(N_SLAB):
            slab = t * N_SLAB + s
            lanes = slice(s * LANES, (s + 1) * LANES)
            nat[slab] = arr[:, lanes]
            for a in range(4):
                x4 = nat.at[slab][pl.ds(a, n4, stride=4), :]
                cls4[4 * slab + a] = x4
                o4[a, :, lanes] = x4.astype(_bf16)
            for r in range(16):
                x16 = cls4.at[4 * slab + r % 4][pl.ds(r // 4, n16, stride=4), :]
                o16[r, :, lanes] = x16.astype(_bf16)


def _in_proj(h, mod3, g, w_in):
    B, S, D = h.shape
    tm = TM_PROJ
    nat = jax.ShapeDtypeStruct((B, S, D_ATTN), _bf16)
    nat_spec = pl.BlockSpec((None, tm, D_ATTN), lambda b, i: (b, i, 0))

    def cls_shape(dil):
        return jax.ShapeDtypeStruct((B, dil, S // dil, D_ATTN), _bf16)

    def cls_spec(dil):
        return pl.BlockSpec((None, dil, tm // dil, D_ATTN), lambda b, i: (b, 0, i, 0))

    return pl.pallas_call(
        _inproj_kernel,
        grid=(B, S // tm),
        in_specs=[pl.BlockSpec((None, tm, D), lambda b, i: (b, i, 0)),
                  pl.BlockSpec((None, N_MOD, D), lambda b, i: (b, 0, 0)),
                  _resident((1, D)),
                  _resident(w_in.shape)],
        out_specs=[nat_spec, nat_spec, nat_spec, nat_spec,
                   cls_spec(4), cls_spec(4), cls_spec(4),
                   cls_spec(16), cls_spec(16), cls_spec(16)],
        out_shape=[nat, nat, nat, nat,
                   cls_shape(4), cls_shape(4), cls_shape(4),
                   cls_shape(16), cls_shape(16), cls_shape(16)],
        scratch_shapes=[pltpu.VMEM((tm + POOL_HALO, D_POOL), _f32), pltpu.VMEM((POOL_HALO, D_POOL), _f32),
                        pltpu.VMEM((3 * N_SLAB, tm, LANES), _f32), pltpu.VMEM((3 * N_SLAB * 4, tm // 4, LANES), _f32)],
        compiler_params=_params(2),
        name="in_proj",
    )(h, mod3, g.reshape(1, D), w_in)


QB = 128


def _den_lane(h):
    return N_HEADS + h + (HEAD_DIM if h % 2 == 0 else 0)


def _attn_kernel(q_ref, km_ref, kh_ref, vm_ref, vh_ref, o_ref, st_ref, kbuf, vbuf, bias, *, dil, seq_len):
    b, i = pl.program_id(0), pl.program_id(1)
    tq = q_ref.shape[0]

    @pl.when((b == 0) & (i == 0))
    def _():
        row = jax.lax.broadcasted_iota(jnp.int32, (QB, 2 * QB), 0)
        col = jax.lax.broadcasted_iota(jnp.int32, (QB, 2 * QB), 1)
        delta = row - col + QB
        valid = (delta >= 0) & (delta <= WINDOW)
        dist = (delta * dil).astype(_f32)
        for h in range(N_HEADS):
            slope = 2.0 ** (-8.0 * (h + 1) / N_HEADS) * LOG2E
            bias[h] = jnp.where(valid, -slope * dist, NEG)

    kbuf[0:QB, :] = kh_ref[...]
    kbuf[QB:, :] = km_ref[0:QB, :]
    own_even = jax.lax.broadcasted_iota(jnp.int32, (1, D_ATTN), 1) % LANES < HEAD_DIM
    for dst, src in ((slice(0, QB), vh_ref), (slice(QB, None), vm_ref)):
        v = src[...]
        ones = jnp.ones_like(v)
        vbuf[0, dst, :] = jnp.where(own_even, v, ones)
        vbuf[1, dst, :] = jnp.where(own_even, ones, v)

    lane = jax.lax.broadcasted_iota(jnp.int32, (QB, LANES), 1)
    low_half = lane < HEAD_DIM
    prev_half = jax.lax.broadcasted_iota(jnp.int32, (1, 2 * QB), 1) < QB
    if seq_len >= tq:
        seq_start = {0: jnp.where(prev_half & (i % (seq_len // tq) == 0), NEG, 0.0)}
    else:
        seq_start = {j: jnp.where(prev_half, NEG, 0.0) for j in range(0, tq // QB, seq_len // QB)}

    for j in range(tq // QB):
        rows = slice(j * QB, (j + 1) * QB)
        stats = jnp.zeros((QB, LANES), _f32)
        for hp in range(N_HEADS // 2):
            cs = slice(hp * LANES, (hp + 1) * LANES)
            q2 = q_ref[rows, cs]
            k2 = kbuf[:, cs] if j == 0 else km_ref[(j - 1) * QB:(j + 1) * QB, cs]
            halves = []
            for e in range(2):
                h = 2 * hp + e
                qm = jnp.where(low_half if e == 0 else ~low_half, q2, jnp.zeros_like(q2))
                s = jax.lax.dot_general(qm, k2, (((1,), (1,)), ((), ())), preferred_element_type=_f32)
                s = s + bias[h]
                if j in seq_start:
                    s = s + seq_start[j]
                m = jnp.max(s, axis=-1, keepdims=True)
                p = jnp.exp2(s - m).astype(_bf16)
                pv = jnp.dot(p, vbuf[e, j * QB:(j + 2) * QB, cs], preferred_element_type=_f32)
                halves.append(pv)
                stats = jnp.where(lane == h, m, stats)
                stats = jnp.where(lane == _den_lane(h), pv, stats)
            o_ref[rows, cs] = jnp.where(low_half, halves[0], halves[1]).astype(_bf16)
        st_ref[rows, :] = stats


def _attention(q, k, v, dil, tq=ATTN_ROWS):
    B, C, L, _ = q.shape
    S = C * L
    flat = [a.reshape(B, S, D_ATTN) for a in (q, k, v)]
    main = pl.BlockSpec((None, tq, D_ATTN), lambda b, i: (b, i, 0))
    halo = pl.BlockSpec((None, QB, D_ATTN), lambda b, i: (b, jnp.maximum(i * (tq // QB) - 1, 0), 0))
    num, stats = pl.pallas_call(
        functools.partial(_attn_kernel, dil=dil, seq_len=L),
        grid=(B, S // tq),
        in_specs=[main, main, halo, main, halo],
        out_specs=[main, pl.BlockSpec((None, tq, LANES), lambda b, i: (b, i, 0))],
        out_shape=[jax.ShapeDtypeStruct((B, S, D_ATTN), _bf16),
                   jax.ShapeDtypeStruct((B, S, LANES), _f32)],
        scratch_shapes=[pltpu.VMEM((2 * QB, D_ATTN), _bf16),
                        pltpu.VMEM((2, tq + QB, D_ATTN), _bf16),
                        pltpu.VMEM((N_HEADS, QB, 2 * QB), _f32)],
        compiler_params=_params(2),
        name=f"attn_d{dil}",
    )(flat[0], flat[1], flat[1], flat[2], flat[2])
    return num.reshape(B, C, L, D_ATTN), stats.reshape(B, C, L, LANES)


N_BRANCH_SLAB = N_SLAB + 1


def _to_natural(o4_ref, s4_ref, o16_ref, s16_ref, nat4, nat16, cls4):
    n4, n16 = nat4.shape[1] // 4, nat4.shape[1] // 16

    def slab_of(o_ref, s_ref, c, s):
        if s < N_SLAB:
            return o_ref[c, :, s * LANES:(s + 1) * LANES].astype(_f32)
        return s_ref[c]

    for s in range(N_BRANCH_SLAB):
        for a in range(4):
            nat4.at[s][pl.ds(a, n4, stride=4), :] = slab_of(o4_ref, s4_ref, a, s)
        for r in range(16):
            cls4.at[N_BRANCH_SLAB * (r % 4) + s][pl.ds(r // 4, n16, stride=4), :] = slab_of(o16_ref, s16_ref, r, s)
    for s in range(N_BRANCH_SLAB):
        for a in range(4):
            nat16.at[s][pl.ds(a, n4, stride=4), :] = cls4[N_BRANCH_SLAB * a + s]


def _pool_chunk(pooled_ref, wp_ref, ps_ref, c):
    rows = slice(c * CHUNK, (c + 1) * CHUNK)
    ys = []
    for g in range(N_POOL_GROUPS):
        cs = slice(g * POOL_GROUP_DIM, (g + 1) * POOL_GROUP_DIM)
        y = jnp.dot(pooled_ref[rows, cs], wp_ref[g], preferred_element_type=_f32) * ps_ref[:, cs]
        ys.append(y.astype(_bf16))
    return jnp.concatenate(ys, axis=1)


def _merge_chunk(o1_ref, s1_ref, nat4, nat16, c):
    rows = slice(c * CHUNK, (c + 1) * CHUNK)
    stats = (s1_ref[rows, :], nat4[N_SLAB, rows, :], nat16[N_SLAB, rows, :])
    top = jnp.maximum(jnp.maximum(stats[0], stats[1]), stats[2])
    es = [jnp.exp2(s - top) for s in stats]
    even_head = jax.lax.broadcasted_iota(jnp.int32, (CHUNK, LANES), 1) % 2 == 0
    dens = [jnp.where(even_head, pltpu.roll(s, LANES - _den_lane(0), 1), pltpu.roll(s, LANES - _den_lane(1) + 1, 1))
            for s in stats]
    inv = 1.0 / (es[0] * dens[0] + es[1] * dens[1] + es[2] * dens[2])
    wts = [e * inv for e in es]
    upper = (jax.lax.broadcasted_iota(jnp.int32, (CHUNK, LANES), 1) >= HEAD_DIM).astype(jnp.int32)
    ys = []
    for hp in range(N_HEADS // 2):
        nums = (o1_ref[rows, hp * LANES:(hp + 1) * LANES].astype(_f32), nat4[hp, rows, :], nat16[hp, rows, :])
        src_lane = 2 * hp + upper
        y = None
        for r in range(3):
            w = jnp.take_along_axis(wts[r], src_lane, axis=1, mode="promise_in_bounds")
            y = w * nums[r] if y is None else y + w * nums[r]
        ys.append(y.astype(_bf16))
    return jnp.concatenate(ys, axis=1)


def _mixffn_kernel(o1_ref, s1_ref, o4_ref, s4_ref, o16_ref, s16_ref, pooled_ref, modn_ref, wp_ref, ps_ref, wo_ref,
                   hn_ref, h_ref, mod_ref, g_ref, wg_ref, wu_ref, wd_ref, gf_ref, out_ref,
                   delta, nbuf, nat4, nat16, cls4, *, final_norm):
    s = pl.program_id(0)
    tm = h_ref.shape[0]
    gate_mix = modn_ref[5:6, :]

    def load_mixer_inputs():
        _to_natural(o4_ref, s4_ref, o16_ref, s16_ref, nat4, nat16, cls4)

    def mixed(c):
        y_pool = _pool_chunk(pooled_ref, wp_ref, ps_ref, c)
        y_attn = _merge_chunk(o1_ref, s1_ref, nat4, nat16, c)
        return jnp.concatenate([y_pool, y_attn], axis=1)

    def project(c, y, slot):
        delta[slot, c * CHUNK:(c + 1) * CHUNK, :] = gate_mix * jnp.dot(y, wo_ref[...], preferred_element_type=_f32)

    def normalise(slot):
        x_next = hn_ref[...] + delta[slot]
        nbuf[slot] = _rms_mod(x_next, g_ref[...], modn_ref[6:7, :], modn_ref[7:8, :]).astype(_bf16)

    n_chunks = tm // CHUNK

    @pl.when(s == 0)
    def _():
        load_mixer_inputs()
        for c in range(n_chunks):
            project(c, mixed(c), 0)
        normalise(0)

    @pl.when(s > 0)
    def _():
        n = nbuf[(s - 1) % 2]
        pending = {}
        after_chunk = {0: load_mixer_inputs, 2 * n_chunks + 3: functools.partial(normalise, s % 2)}
        for c in range(n_chunks):
            after_chunk[1 + 2 * c] = functools.partial(lambda c: pending.__setitem__(c, mixed(c)), c)
            after_chunk[2 + 2 * c] = functools.partial(lambda c: project(c, pending[c], s % 2), c)
        acc = _swiglu(n, wg_ref, wu_ref, wd_ref, after_chunk)
        h = h_ref[...] + delta[(s - 1) % 2] + (0.5 * mod_ref[8:9, :]) * acc
        if final_norm:
            h = h * jax.lax.rsqrt(jnp.mean(h * h, axis=-1, keepdims=True) + EPS) * gf_ref[...]
        out_ref[...] = h


def _mix_ffn(a1, a4, a16, pooled, h, mod3, w_pool, pool_scale, w_out, g, wg, wu, wd, g_final, *, final_norm):
    B, S, D = h.shape
    tm = TM_MIXFFN
    per_b = S // tm
    n_tiles = B * per_b
    assert tm % CHUNK == 0

    def tile(t):
        return t // per_b, t % per_b

    def nxt(s):
        return tile(jnp.minimum(s, n_tiles - 1))

    def cur(s):
        return tile(jnp.maximum(s - 1, 0))

    def branch_spec(dil, width):
        if dil == 1:
            return pl.BlockSpec((None, None, tm, width), lambda s: (nxt(s)[0], 0, nxt(s)[1], 0))
        return pl.BlockSpec((None, dil, tm // dil, width), lambda s: (nxt(s)[0], 0, nxt(s)[1], 0))

    row_tile = pl.BlockSpec((None, tm, D), lambda s: (cur(s)[0], cur(s)[1], 0))
    return pl.pallas_call(
        functools.partial(_mixffn_kernel, final_norm=final_norm),
        grid=(n_tiles + 1,),
        in_specs=[branch_spec(1, D_ATTN), branch_spec(1, LANES),
                  branch_spec(4, D_ATTN), branch_spec(4, LANES),
                  branch_spec(16, D_ATTN), branch_spec(16, LANES),
                  pl.BlockSpec((None, tm, D_POOL), lambda s: (nxt(s)[0], nxt(s)[1], 0)),
                  pl.BlockSpec((None, N_MOD, D), lambda s: (nxt(s)[0], 0, 0)),
                  _resident(w_pool.shape),
                  _resident((1, D_POOL)),
                  _resident(w_out.shape),
                  pl.BlockSpec((None, tm, D), lambda s: (nxt(s)[0], nxt(s)[1], 0)),
                  row_tile,
                  pl.BlockSpec((None, N_MOD, D), lambda s: (cur(s)[0], 0, 0)),
                  _resident((1, D)),
                  _resident(wg.shape),
                  _resident(wu.shape),
                  _resident(wd.shape),
                  _resident((1, D))],
        out_specs=row_tile,
        out_shape=jax.ShapeDtypeStruct((B, S, D), _f32),
        scratch_shapes=[pltpu.VMEM((2, tm, D), _f32),
                        pltpu.VMEM((2, tm, D), _bf16),
                        pltpu.VMEM((N_BRANCH_SLAB, tm, LANES), _f32),
                        pltpu.VMEM((N_BRANCH_SLAB, tm, LANES), _f32),
                        pltpu.VMEM((N_BRANCH_SLAB * 4, tm // 4, LANES), _f32)],
        compiler_params=_params(1),
        name="mix_ffn",
    )(*a1, *a4, *a16, pooled, mod3, w_pool, pool_scale.reshape(1, D_POOL), w_out,
      h, h, mod3, g.reshape(1, D), wg, wu, wd, g_final.reshape(1, D))


def kernel(x, c, w_ada, b_ada, g_ffn1, w1_gate, w1_up, w1_down, g_mix, w_in, w_pool, pool_scale, w_out,
           g_ffn2, w2_gate, w2_up, w2_down, g_final):
    B, S, D = x.shape
    depth = w_ada.shape[0]
    h = x
    for l in range(depth):
        mod, (wg1, wu1, wd1) = _ada_mod(c, w_ada[l], b_ada[l], (w1_gate[l], w1_up[l], w1_down[l]))
        mod3 = mod.reshape(B, N_MOD, D)
        pool_flat = w_pool[l].reshape(N_POOL_GROUPS * POOL_GROUP_DIM, POOL_GROUP_DIM)
        h, (wg2, wu2, wd2, w_in_b, w_out_b, w_pool_b) = _ffn(
            h, mod3, g_ffn1[l], wg1, wu1, wd1,
            narrow=(w2_gate[l], w2_up[l], w2_down[l], w_in[l], w_out[l], pool_flat))
        pooled, q1, k1, v1, q4, k4, v4, q16, k16, v16 = _in_proj(h, mod3, g_mix[l], w_in_b)
        a1 = _attention(q1[:, None], k1[:, None], v1[:, None], 1)
        a4 = _attention(q4, k4, v4, 4, tq=ATTN_ROWS // 2)
        a16 = _attention(q16, k16, v16, 16)
        h = _mix_ffn(a1, a4, a16, pooled, h, mod3, w_pool_b.reshape(w_pool[l].shape), pool_scale[l], w_out_b,
                     g_ffn2[l], wg2, wu2, wd2, g_final, final_norm=(l == depth - 1))
    return h
```

```python
import functools
import math

import jax
import jax.numpy as jnp
import numpy as np
from jax.experimental import pallas as pl
from jax.experimental.pallas import tpu as pltpu

D_MODEL = 1024
D_POOL = 512
D_ATTN = 512
POOL_WINDOWS = (2, 4, 8, 16)
POOL_GROUP_DIM = 128
N_POOL_GROUPS = len(POOL_WINDOWS)
HEAD_DIM = 64
N_HEADS = 8
WINDOW = 128
D_FF = 2816
N_MOD = 9
EPS = 1e-6
NEG = -1e30
LOG2E = math.log2(math.e)

LANES = 128
PERM_ROWS = 256
CHUNK = 256
VMEM_LIMIT = 56 * 1024 * 1024

TM_FFN = 1024
TM_PROJ = 1024
TM_MIXFFN = 512
ATTN_ROWS = 2048
BN_ADA = 1152

_f32 = jnp.float32
_bf16 = jnp.bfloat16


def _rms_mod(x, g, shift, scale):
    r = jax.lax.rsqrt(jnp.mean(x * x, axis=-1, keepdims=True) + EPS)
    return x * r * (g * (1.0 + scale)) + shift


def _resident(shape):
    nd = len(shape)
    return pl.BlockSpec(shape, lambda *_: (0,) * nd, pipeline_mode=pl.Buffered(1))


def _params(n_axes):
    return pltpu.CompilerParams(dimension_semantics=("arbitrary",) * n_axes, vmem_limit_bytes=VMEM_LIMIT)


def _ada_kernel(c_ref, w_ref, b_ref, *refs):
    n_cast = len(refs) // 2
    o_ref = refs[n_cast]
    c = c_ref[...]
    a = (c * (1.0 / (1.0 + jnp.exp(-c)))).astype(_bf16)
    o_ref[...] = jnp.dot(a, w_ref[...].astype(_bf16), preferred_element_type=_f32) + b_ref[...]
    for src, dst in zip(refs[:n_cast], refs[n_cast + 1:]):
        dst[...] = src[...].astype(_bf16)


def _ada_mod(c, w_ada, b_ada, narrow):
    B, D = c.shape
    N = w_ada.shape[1]
    steps = N // BN_ADA

    def cast_spec(w):
        return pl.BlockSpec((w.shape[0] // steps, w.shape[1]), lambda j: (j, 0))

    outs = pl.pallas_call(
        _ada_kernel,
        grid=(steps,),
        in_specs=[pl.BlockSpec((B, D), lambda j: (0, 0)),
                  pl.BlockSpec((D, BN_ADA), lambda j: (0, j)),
                  pl.BlockSpec((1, BN_ADA), lambda j: (0, j))] + [cast_spec(w) for w in narrow],
        out_specs=[pl.BlockSpec((B, BN_ADA), lambda j: (0, j))] + [cast_spec(w) for w in narrow],
        out_shape=[jax.ShapeDtypeStruct((B, N), _f32)] + [jax.ShapeDtypeStruct(w.shape, _bf16) for w in narrow],
        compiler_params=_params(1),
        name="ada_mod",
    )(c, w_ada, b_ada.reshape(1, N), *narrow)
    return outs[0], outs[1:]


FF_CHUNK = 256
N_CAST_BLOCKS = 16


def _swiglu(n, wg_ref, wu_ref, wd_ref, after_chunk=None):
    acc = jnp.zeros((n.shape[0], wd_ref.shape[1]), _f32)
    for c in range(D_FF // FF_CHUNK):
        cs = slice(c * FF_CHUNK, (c + 1) * FF_CHUNK)
        g = jnp.dot(n, wg_ref[:, cs], preferred_element_type=_f32)
        u = jnp.dot(n, wu_ref[:, cs], preferred_element_type=_f32)
        a = (g * (1.0 / (1.0 + jnp.exp(-g))) * u).astype(_bf16)
        acc = acc + jnp.dot(a, wd_ref[cs, :], preferred_element_type=_f32)
        if after_chunk and c in after_chunk:
            after_chunk[c]()
    return acc


def _ffn_kernel(x_ref, mod_ref, g_ref, wg_ref, wu_ref, wd_ref, *refs):
    n_cast = len(refs) // 2
    o_ref = refs[n_cast]
    x = x_ref[...]
    n = _rms_mod(x, g_ref[...], mod_ref[0:1, :], mod_ref[1:2, :]).astype(_bf16)
    for src, dst in zip(refs[:n_cast], refs[n_cast + 1:]):
        dst[...] = src[...].astype(_bf16)
    o_ref[...] = x + (0.5 * mod_ref[2:3, :]) * _swiglu(n, wg_ref, wu_ref, wd_ref)


def _ffn(x, mod3, g, wg, wu, wd, narrow):
    B, S, D = x.shape
    tm = TM_FFN
    steps = B * (S // tm)
    per_b = S // tm

    def cast_spec(w):
        every = steps // N_CAST_BLOCKS
        return pl.BlockSpec((w.shape[0] // N_CAST_BLOCKS, w.shape[1]),
                            lambda b, i: ((b * per_b + i) // every, 0))

    row_tile = pl.BlockSpec((None, tm, D), lambda b, i: (b, i, 0))
    outs = pl.pallas_call(
        _ffn_kernel,
        grid=(B, per_b),
        in_specs=[row_tile,
                  pl.BlockSpec((None, N_MOD, D), lambda b, i: (b, 0, 0)),
                  _resident((1, D)),
                  _resident(wg.shape),
                  _resident(wu.shape),
                  _resident(wd.shape)] + [cast_spec(w) for w in narrow],
        out_specs=[row_tile] + [cast_spec(w) for w in narrow],
        out_shape=[jax.ShapeDtypeStruct((B, S, D), _f32)]
                  + [jax.ShapeDtypeStruct(w.shape, _bf16) for w in narrow],
        compiler_params=_params(2),
        name="ffn",
    )(x, mod3, g.reshape(1, D), wg, wu, wd, *narrow)
    return outs[0], outs[1:]


def _sort_matrix(dil):
    per = PERM_ROWS // dil
    i = np.arange(PERM_ROWS)
    src = (i % per) * dil + i // per
    p = np.zeros((PERM_ROWS, PERM_ROWS), np.float32)
    p[i, src] = 1.0
    return p


def _inproj_kernel(h_ref, mod_ref, g_ref, w_ref, p4_ref, p16_ref,
                   u_ref, q1_ref, k1_ref, v1_ref, q4_ref, k4_ref, v4_ref, q16_ref, k16_ref, v16_ref):
    x = h_ref[...]
    tm = x.shape[0]
    n = _rms_mod(x, g_ref[...], mod_ref[3:4, :], mod_ref[4:5, :]).astype(_bf16)
    z = jnp.dot(n, w_ref[...], preferred_element_type=_f32)
    u_ref[...] = z[:, :D_POOL]
    q = (z[:, D_POOL:D_POOL + D_ATTN] * (HEAD_DIM ** -0.5 * LOG2E)).astype(_bf16)
    k = z[:, D_POOL + D_ATTN:D_POOL + 2 * D_ATTN].astype(_bf16)
    v = z[:, D_POOL + 2 * D_ATTN:].astype(_bf16)
    q1_ref[...] = q
    k1_ref[...] = k
    v1_ref[...] = v
    qkv = jnp.concatenate([q, k, v], axis=1)
    for dil, p_ref, outs in ((4, p4_ref, (q4_ref, k4_ref, v4_ref)),
                             (16, p16_ref, (q16_ref, k16_ref, v16_ref))):
        per = PERM_ROWS // dil
        for c in range(tm // PERM_ROWS):
            s = jnp.dot(p_ref[...], qkv[c * PERM_ROWS:(c + 1) * PERM_ROWS, :],
                        preferred_element_type=_f32).astype(_bf16)
            for a, o_ref in enumerate(outs):
                for cls in range(dil):
                    o_ref[cls, c * per:(c + 1) * per, :] = s[cls * per:(cls + 1) * per,
                                                             a * D_ATTN:(a + 1) * D_ATTN]


def _in_proj(h, mod3, g, w_in, p4, p16):
    B, S, D = h.shape
    tm = TM_PROJ
    nat = jax.ShapeDtypeStruct((B, S, D_ATTN), _bf16)
    nat_spec = pl.BlockSpec((None, tm, D_ATTN), lambda b, i: (b, i, 0))

    def cls_shape(dil):
        return jax.ShapeDtypeStruct((B, dil, S // dil, D_ATTN), _bf16)

    def cls_spec(dil):
        return pl.BlockSpec((None, dil, tm // dil, D_ATTN), lambda b, i: (b, 0, i, 0))

    return pl.pallas_call(
        _inproj_kernel,
        grid=(B, S // tm),
        in_specs=[pl.BlockSpec((None, tm, D), lambda b, i: (b, i, 0)),
                  pl.BlockSpec((None, N_MOD, D), lambda b, i: (b, 0, 0)),
                  _resident((1, D)),
                  _resident(w_in.shape),
                  _resident(p4.shape),
                  _resident(p16.shape)],
        out_specs=[pl.BlockSpec((None, tm, D_POOL), lambda b, i: (b, i, 0)),
                   nat_spec, nat_spec, nat_spec,
                   cls_spec(4), cls_spec(4), cls_spec(4),
                   cls_spec(16), cls_spec(16), cls_spec(16)],
        out_shape=[jax.ShapeDtypeStruct((B, S, D_POOL), _f32), nat, nat, nat,
                   cls_shape(4), cls_shape(4), cls_shape(4),
                   cls_shape(16), cls_shape(16), cls_shape(16)],
        compiler_params=_params(2),
        name="in_proj",
    )(h, mod3, g.reshape(1, D), w_in, p4, p16)


QB = 128


def _den_lane(h):
    return N_HEADS + h + (HEAD_DIM if h % 2 == 0 else 0)


def _attn_kernel(q_ref, km_ref, kh_ref, vm_ref, vh_ref, o_ref, st_ref, kbuf, vbuf, bias, *, dil, seq_len):
    b, i = pl.program_id(0), pl.program_id(1)
    tq = q_ref.shape[0]

    @pl.when((b == 0) & (i == 0))
    def _():
        row = jax.lax.broadcasted_iota(jnp.int32, (QB, 2 * QB), 0)
        col = jax.lax.broadcasted_iota(jnp.int32, (QB, 2 * QB), 1)
        delta = row - col + QB
        valid = (delta >= 0) & (delta <= WINDOW)
        dist = (delta * dil).astype(_f32)
        for h in range(N_HEADS):
            slope = 2.0 ** (-8.0 * (h + 1) / N_HEADS) * LOG2E
            bias[h] = jnp.where(valid, -slope * dist, NEG)

    kbuf[0:QB, :] = kh_ref[...]
    kbuf[QB:, :] = km_ref[0:QB, :]
    own_even = jax.lax.broadcasted_iota(jnp.int32, (1, D_ATTN), 1) % LANES < HEAD_DIM
    for dst, src in ((slice(0, QB), vh_ref), (slice(QB, None), vm_ref)):
        v = src[...]
        ones = jnp.ones_like(v)
        vbuf[0, dst, :] = jnp.where(own_even, v, ones)
        vbuf[1, dst, :] = jnp.where(own_even, ones, v)

    lane = jax.lax.broadcasted_iota(jnp.int32, (QB, LANES), 1)
    low_half = lane < HEAD_DIM
    prev_half = jax.lax.broadcasted_iota(jnp.int32, (1, 2 * QB), 1) < QB
    if seq_len >= tq:
        seq_start = {0: jnp.where(prev_half & (i % (seq_len // tq) == 0), NEG, 0.0)}
    else:
        seq_start = {j: jnp.where(prev_half, NEG, 0.0) for j in range(0, tq // QB, seq_len // QB)}

    for j in range(tq // QB):
        rows = slice(j * QB, (j + 1) * QB)
        stats = jnp.zeros((QB, LANES), _f32)
        for hp in range(N_HEADS // 2):
            cs = slice(hp * LANES, (hp + 1) * LANES)
            q2 = q_ref[rows, cs]
            k2 = kbuf[:, cs] if j == 0 else km_ref[(j - 1) * QB:(j + 1) * QB, cs]
            halves = []
            for e in range(2):
                h = 2 * hp + e
                qm = jnp.where(low_half if e == 0 else ~low_half, q2, jnp.zeros_like(q2))
                s = jax.lax.dot_general(qm, k2, (((1,), (1,)), ((), ())), preferred_element_type=_f32)
                s = s + bias[h]
                if j in seq_start:
                    s = s + seq_start[j]
                m = jnp.max(s, axis=-1, keepdims=True)
                p = jnp.exp2(s - m).astype(_bf16)
                pv = jnp.dot(p, vbuf[e, j * QB:(j + 2) * QB, cs], preferred_element_type=_f32)
                halves.append(pv)
                stats = jnp.where(lane == h, m, stats)
                stats = jnp.where(lane == _den_lane(h), pv, stats)
            o_ref[rows, cs] = jnp.where(low_half, halves[0], halves[1]).astype(_bf16)
        st_ref[rows, :] = stats


def _attention(q, k, v, dil, tq=ATTN_ROWS):
    B, C, L, _ = q.shape
    S = C * L
    flat = [a.reshape(B, S, D_ATTN) for a in (q, k, v)]
    main = pl.BlockSpec((None, tq, D_ATTN), lambda b, i: (b, i, 0))
    halo = pl.BlockSpec((None, QB, D_ATTN), lambda b, i: (b, jnp.maximum(i * (tq // QB) - 1, 0), 0))
    num, stats = pl.pallas_call(
        functools.partial(_attn_kernel, dil=dil, seq_len=L),
        grid=(B, S // tq),
        in_specs=[main, main, halo, main, halo],
        out_specs=[main, pl.BlockSpec((None, tq, LANES), lambda b, i: (b, i, 0))],
        out_shape=[jax.ShapeDtypeStruct((B, S, D_ATTN), _bf16),
                   jax.ShapeDtypeStruct((B, S, LANES), _f32)],
        scratch_shapes=[pltpu.VMEM((2 * QB, D_ATTN), _bf16),
                        pltpu.VMEM((2, tq + QB, D_ATTN), _bf16),
                        pltpu.VMEM((N_HEADS, QB, 2 * QB), _f32)],
        compiler_params=_params(2),
        name=f"attn_d{dil}",
    )(flat[0], flat[1], flat[1], flat[2], flat[2])
    return num.reshape(B, C, L, D_ATTN), stats.reshape(B, C, L, LANES)


POOL_HALO = 16
N_SLAB = D_ATTN // LANES
N_BRANCH_SLAB = N_SLAB + 1


def _to_natural(o4_ref, s4_ref, o16_ref, s16_ref, nat4, nat16, cls4):
    n4, n16 = nat4.shape[1] // 4, nat4.shape[1] // 16

    def slab_of(o_ref, s_ref, c, s):
        if s < N_SLAB:
            return o_ref[c, :, s * LANES:(s + 1) * LANES].astype(_f32)
        return s_ref[c]

    for s in range(N_BRANCH_SLAB):
        for a in range(4):
            nat4.at[s][pl.ds(a, n4, stride=4), :] = slab_of(o4_ref, s4_ref, a, s)
        for r in range(16):
            cls4.at[N_BRANCH_SLAB * (r % 4) + s][pl.ds(r // 4, n16, stride=4), :] = slab_of(o16_ref, s16_ref, r, s)
    for s in range(N_BRANCH_SLAB):
        for a in range(4):
            nat16.at[s][pl.ds(a, n4, stride=4), :] = cls4[N_BRANCH_SLAB * a + s]


def _pool_chunk(ubuf, wp_ref, ps_ref, c, t0):
    ext = CHUNK + POOL_HALO
    t = t0 + jax.lax.broadcasted_iota(jnp.int32, (CHUNK, 1), 0)
    ys = []
    for g, w in enumerate(POOL_WINDOWS):
        cs = slice(g * POOL_GROUP_DIM, (g + 1) * POOL_GROUP_DIM)
        x = ubuf[c * CHUNK:c * CHUNK + ext, cs]
        win = x
        back = 1
        while back < w:
            win = win + pltpu.roll(win, back, 0)
            back *= 2
        count = jnp.minimum(t + 1, w).astype(_f32)
        tok = x[POOL_HALO:, :]
        pooled = (win[POOL_HALO:, :] / count - tok).astype(_bf16)
        y = jnp.dot(pooled, wp_ref[g], preferred_element_type=_f32) * ps_ref[:, cs]
        ys.append(y.astype(_bf16))
    return jnp.concatenate(ys, axis=1)


def _merge_chunk(o1_ref, s1_ref, nat4, nat16, c):
    rows = slice(c * CHUNK, (c + 1) * CHUNK)
    stats = (s1_ref[rows, :], nat4[N_SLAB, rows, :], nat16[N_SLAB, rows, :])
    top = jnp.maximum(jnp.maximum(stats[0], stats[1]), stats[2])
    es = [jnp.exp2(s - top) for s in stats]
    even_head = jax.lax.broadcasted_iota(jnp.int32, (CHUNK, LANES), 1) % 2 == 0
    dens = [jnp.where(even_head, pltpu.roll(s, LANES - _den_lane(0), 1), pltpu.roll(s, LANES - _den_lane(1) + 1, 1))
            for s in stats]
    inv = 1.0 / (es[0] * dens[0] + es[1] * dens[1] + es[2] * dens[2])
    wts = [e * inv for e in es]
    upper = (jax.lax.broadcasted_iota(jnp.int32, (CHUNK, LANES), 1) >= HEAD_DIM).astype(jnp.int32)
    ys = []
    for hp in range(N_HEADS // 2):
        nums = (o1_ref[rows, hp * LANES:(hp + 1) * LANES].astype(_f32), nat4[hp, rows, :], nat16[hp, rows, :])
        src_lane = 2 * hp + upper
        y = None
        for r in range(3):
            w = jnp.take_along_axis(wts[r], src_lane, axis=1, mode="promise_in_bounds")
            y = w * nums[r] if y is None else y + w * nums[r]
        ys.append(y.astype(_bf16))
    return jnp.concatenate(ys, axis=1)


def _mixffn_kernel(o1_ref, s1_ref, o4_ref, s4_ref, o16_ref, s16_ref, u_ref, uh_ref, modn_ref, wp_ref, ps_ref, wo_ref,
                   hn_ref, h_ref, mod_ref, g_ref, wg_ref, wu_ref, wd_ref, gf_ref, out_ref,
                   delta, nbuf, ubuf, nat4, nat16, cls4, *, final_norm, per_b):
    s = pl.program_id(0)
    tm = h_ref.shape[0]
    last = pl.num_programs(0) - 2
    i_next = jnp.minimum(s, last) % per_b
    gate_mix = modn_ref[5:6, :]

    def load_mixer_inputs():
        ubuf[0:POOL_HALO, :] = jnp.where(i_next == 0, 0.0, uh_ref[...])
        ubuf[POOL_HALO:, :] = u_ref[...]
        _to_natural(o4_ref, s4_ref, o16_ref, s16_ref, nat4, nat16, cls4)

    def mixed(c):
        y_pool = _pool_chunk(ubuf, wp_ref, ps_ref, c, i_next * tm + c * CHUNK)
        y_attn = _merge_chunk(o1_ref, s1_ref, nat4, nat16, c)
        return jnp.concatenate([y_pool, y_attn], axis=1)

    def project(c, y, slot):
        delta[slot, c * CHUNK:(c + 1) * CHUNK, :] = gate_mix * jnp.dot(y, wo_ref[...], preferred_element_type=_f32)

    def normalise(slot):
        x_next = hn_ref[...] + delta[slot]
        nbuf[slot] = _rms_mod(x_next, g_ref[...], modn_ref[6:7, :], modn_ref[7:8, :]).astype(_bf16)

    n_chunks = tm // CHUNK

    @pl.when(s == 0)
    def _():
        load_mixer_inputs()
        for c in range(n_chunks):
            project(c, mixed(c), 0)
        normalise(0)

    @pl.when(s > 0)
    def _():
        n = nbuf[(s - 1) % 2]
        pending = {}
        after_chunk = {0: load_mixer_inputs, 2 * n_chunks + 3: functools.partial(normalise, s % 2)}
        for c in range(n_chunks):
            after_chunk[1 + 2 * c] = functools.partial(lambda c: pending.__setitem__(c, mixed(c)), c)
            after_chunk[2 + 2 * c] = functools.partial(lambda c: project(c, pending[c], s % 2), c)
        acc = _swiglu(n, wg_ref, wu_ref, wd_ref, after_chunk)
        h = h_ref[...] + delta[(s - 1) % 2] + (0.5 * mod_ref[8:9, :]) * acc
        if final_norm:
            h = h * jax.lax.rsqrt(jnp.mean(h * h, axis=-1, keepdims=True) + EPS) * gf_ref[...]
        out_ref[...] = h


def _mix_ffn(a1, a4, a16, u, h, mod3, w_pool, pool_scale, w_out, g, wg, wu, wd, g_final, *, final_norm):
    B, S, D = h.shape
    tm = TM_MIXFFN
    per_b = S // tm
    n_tiles = B * per_b
    assert tm % CHUNK == 0

    def tile(t):
        return t // per_b, t % per_b

    def nxt(s):
        return tile(jnp.minimum(s, n_tiles - 1))

    def cur(s):
        return tile(jnp.maximum(s - 1, 0))

    def branch_spec(dil, width):
        if dil == 1:
            return pl.BlockSpec((None, None, tm, width), lambda s: (nxt(s)[0], 0, nxt(s)[1], 0))
        return pl.BlockSpec((None, dil, tm // dil, width), lambda s: (nxt(s)[0], 0, nxt(s)[1], 0))

    row_tile = pl.BlockSpec((None, tm, D), lambda s: (cur(s)[0], cur(s)[1], 0))
    return pl.pallas_call(
        functools.partial(_mixffn_kernel, final_norm=final_norm, per_b=per_b),
        grid=(n_tiles + 1,),
        in_specs=[branch_spec(1, D_ATTN), branch_spec(1, LANES),
                  branch_spec(4, D_ATTN), branch_spec(4, LANES),
                  branch_spec(16, D_ATTN), branch_spec(16, LANES),
                  pl.BlockSpec((None, tm, D_POOL), lambda s: (nxt(s)[0], nxt(s)[1], 0)),
                  pl.BlockSpec((None, POOL_HALO, D_POOL),
                               lambda s: (nxt(s)[0], jnp.maximum(nxt(s)[1] * (tm // POOL_HALO) - 1, 0), 0)),
                  pl.BlockSpec((None, N_MOD, D), lambda s: (nxt(s)[0], 0, 0)),
                  _resident(w_pool.shape),
                  _resident((1, D_POOL)),
                  _resident(w_out.shape),
                  pl.BlockSpec((None, tm, D), lambda s: (nxt(s)[0], nxt(s)[1], 0)),
                  row_tile,
                  pl.BlockSpec((None, N_MOD, D), lambda s: (cur(s)[0], 0, 0)),
                  _resident((1, D)),
                  _resident(wg.shape),
                  _resident(wu.shape),
                  _resident(wd.shape),
                  _resident((1, D))],
        out_specs=row_tile,
        out_shape=jax.ShapeDtypeStruct((B, S, D), _f32),
        scratch_shapes=[pltpu.VMEM((2, tm, D), _f32),
                        pltpu.VMEM((2, tm, D), _bf16),
                        pltpu.VMEM((tm + POOL_HALO, D_POOL), _f32),
                        pltpu.VMEM((N_BRANCH_SLAB, tm, LANES), _f32),
                        pltpu.VMEM((N_BRANCH_SLAB, tm, LANES), _f32),
                        pltpu.VMEM((N_BRANCH_SLAB * 4, tm // 4, LANES), _f32)],
        compiler_params=_params(1),
        name="mix_ffn",
    )(*a1, *a4, *a16, u, u, mod3, w_pool, pool_scale.reshape(1, D_POOL), w_out,
      h, h, mod3, g.reshape(1, D), wg, wu, wd, g_final.reshape(1, D))


def kernel(x, c, w_ada, b_ada, g_ffn1, w1_gate, w1_up, w1_down, g_mix, w_in, w_pool, pool_scale, w_out,
           g_ffn2, w2_gate, w2_up, w2_down, g_final):
    B, S, D = x.shape
    depth = w_ada.shape[0]
    p4 = jnp.asarray(_sort_matrix(4), _bf16)
    p16 = jnp.asarray(_sort_matrix(16), _bf16)
    h = x
    for l in range(depth):
        mod, (wg1, wu1, wd1) = _ada_mod(c, w_ada[l], b_ada[l], (w1_gate[l], w1_up[l], w1_down[l]))
        mod3 = mod.reshape(B, N_MOD, D)
        pool_flat = w_pool[l].reshape(N_POOL_GROUPS * POOL_GROUP_DIM, POOL_GROUP_DIM)
        h, (wg2, wu2, wd2, w_in_b, w_out_b, w_pool_b) = _ffn(
            h, mod3, g_ffn1[l], wg1, wu1, wd1,
            narrow=(w2_gate[l], w2_up[l], w2_down[l], w_in[l], w_out[l], pool_flat))
        u, q1, k1, v1, q4, k4, v4, q16, k16, v16 = _in_proj(h, mod3, g_mix[l], w_in_b, p4, p16)
        a1 = _attention(q1[:, None], k1[:, None], v1[:, None], 1)
        a4 = _attention(q4, k4, v4, 4, tq=2 * ATTN_ROWS)
        a16 = _attention(q16, k16, v16, 16)
        h = _mix_ffn(a1, a4, a16, u, h, mod3, w_pool_b.reshape(w_pool[l].shape), pool_scale[l], w_out_b,
                     g_ffn2[l], wg2, wu2, wd2, g_final, final_norm=(l == depth - 1))
    return h
```
